```python
import jax, jax.numpy as jnp
from jax import lax
import numpy as np

D_MODEL = 4096
BATCH = 4
SEQ = 2048
DEPTH = 2
DEC_BATCH = 2
DEC_SEQ = 4096
PAST_LEN = 128

N_ATT_HEADS = D_MODEL // 256
QK_NOPE_DIM = 128
QK_ROPE_DIM = 64
V_HEAD_DIM = 128
Q_LORA_RANK = D_MODEL // 4
KV_LORA_RANK = D_MODEL // 8
ROPE_THETA = 10000.0
Q_BLOCK = 128
D_ATT = N_ATT_HEADS * V_HEAD_DIM
ATT_SCALE = (QK_NOPE_DIM + QK_ROPE_DIM) ** -0.5
D_RNN = D_MODEL // 2
RNN_BLOCKS = D_MODEL // 256
RNN_BLOCK_W = D_RNN // RNN_BLOCKS
CONV_WIDTH = 4
CONV_PAD = (2, 1)
RG_C = 8.0
D_MIX = D_ATT + D_RNN
IN_SPLITS = (Q_LORA_RANK, Q_LORA_RANK + KV_LORA_RANK, Q_LORA_RANK + KV_LORA_RANK + QK_ROPE_DIM, Q_LORA_RANK + KV_LORA_RANK + QK_ROPE_DIM + D_RNN)
IN_COLS = IN_SPLITS[-1] + D_RNN
D_FF = 7 * D_MODEL // 2
N_EXPERTS = 8
TOP_K = 2
EXPERT_BLOCK = 256
N_DENSE = (DEPTH + 1) // 2
N_MOE = DEPTH // 2
EPS = 1e-6

kernel_name = 'hybrid_rglru_mla_bidir_encoder'


def rmsnorm(x, g):
    xf = x.astype(jnp.float32)
    y = xf * lax.rsqrt(jnp.mean(xf * xf, axis=-1, keepdims=True) + EPS)
    return (y * g.astype(jnp.float32)).astype(x.dtype)


def rope_tables(S, dtype):
    inv = 1.0 / (ROPE_THETA ** (jnp.arange(0, QK_ROPE_DIM, 2, dtype=jnp.float32) / QK_ROPE_DIM))
    ang = jnp.arange(S, dtype=jnp.float32)[:, None] * inv[None, :]
    return jnp.cos(ang).astype(dtype), jnp.sin(ang).astype(dtype)


def apply_rope(x, cos, sin):
    x1, x2 = jnp.split(x, 2, axis=-1)
    return jnp.concatenate([x1 * cos - x2 * sin, x2 * cos + x1 * sin], axis=-1)


def mla_attend(q_nope, q_rope, k_nope, k_rope, v):
    B, S, H, _ = q_nope.shape
    nq = S // Q_BLOCK
    qn = jnp.moveaxis(q_nope.reshape(B, nq, Q_BLOCK, H, QK_NOPE_DIM), 1, 0)
    qr = jnp.moveaxis(q_rope.reshape(B, nq, Q_BLOCK, H, QK_ROPE_DIM), 1, 0)

    def block(args):
        qn_b, qr_b = args
        s = (jnp.einsum('bqhd,bkhd->bhqk', qn_b, k_nope, preferred_element_type=jnp.float32)
             + jnp.einsum('bqhr,bkr->bhqk', qr_b, k_rope, preferred_element_type=jnp.float32))
        p = jax.nn.softmax(s * ATT_SCALE, axis=-1).astype(v.dtype)
        return jnp.einsum('bhqk,bkhd->bqhd', p, v)

    o = lax.map(block, (qn, qr))
    return jnp.moveaxis(o, 0, 1).reshape(B, S, H * V_HEAD_DIM)


def linear_scan(a, b, reverse):
    def combine(left, right):
        a_l, b_l = left
        a_r, b_r = right
        return a_l * a_r, a_r * b_l + b_r
    _, h = lax.associative_scan(combine, (a, b), reverse=reverse, axis=1)
    return h


def rglru_branch(r_x, r_g, conv_w, conv_b, w_a, b_a, w_i, b_i, lam):
    B, S, _ = r_x.shape
    xc = lax.conv_general_dilated(r_x, conv_w[:, None, :].astype(r_x.dtype), window_strides=(1,),
                                  padding=[CONV_PAD], dimension_numbers=('NWC', 'WIO', 'NWC'),
                                  feature_group_count=D_RNN) + conv_b.astype(r_x.dtype)
    xb = xc.reshape(B, S, RNN_BLOCKS, RNN_BLOCK_W)
    za = jnp.einsum('bsnj,dnji->dbsni', xb, w_a).reshape(2, B, S, D_RNN) + b_a[:, None, None, :]
    zi = jnp.einsum('bsnj,dnji->dbsni', xb, w_i).reshape(2, B, S, D_RNN) + b_i[:, None, None, :]
    r = jax.nn.sigmoid(za.astype(jnp.float32))
    i = jax.nn.sigmoid(zi.astype(jnp.float32))
    log_a = -RG_C * r * jax.nn.softplus(-lam.astype(jnp.float32))[:, None, None, :]
    a = jnp.exp(log_a)
    b = jnp.sqrt(-jnp.expm1(2.0 * log_a)) * i * xc.astype(jnp.float32)[None]
    h = linear_scan(a[0], b[0], False) + linear_scan(a[1], b[1], True)
    return (jax.nn.gelu(r_g.astype(jnp.float32)) * h).astype(r_x.dtype)


def mixer(h, p, l):
    B, S, _ = h.shape
    proj = h @ p['w_in'][l]
    q_lat, kv_lat, k_rope, r_x, r_g = jnp.split(proj, IN_SPLITS, axis=-1)
    q = (rmsnorm(q_lat, p['q_norm'][l]) @ p['w_q_up'][l]).reshape(B, S, N_ATT_HEADS, QK_NOPE_DIM + QK_ROPE_DIM)
    q_nope, q_rope = q[..., :QK_NOPE_DIM], q[..., QK_NOPE_DIM:]
    kv = (rmsnorm(kv_lat, p['kv_norm'][l]) @ p['w_kv_up'][l]).reshape(B, S, N_ATT_HEADS, QK_NOPE_DIM + V_HEAD_DIM)
    k_nope, v = kv[..., :QK_NOPE_DIM], kv[..., QK_NOPE_DIM:]
    cos, sin = rope_tables(S, h.dtype)
    q_rope = apply_rope(q_rope, cos[:, None, :], sin[:, None, :])
    k_rope = apply_rope(k_rope, cos, sin)
    y_att = mla_attend(q_nope, q_rope, k_nope, k_rope, v)
    y_rnn = rglru_branch(r_x, r_g, p['conv_w'][l], p['conv_b'][l], p['w_rg_a'][l], p['b_rg_a'][l],
                         p['w_rg_i'][l], p['b_rg_i'][l], p['rg_lambda'][l])
    y = jnp.concatenate([rmsnorm(y_att, p['attn_out_norm'][l]), rmsnorm(y_rnn, p['rnn_out_norm'][l])], axis=-1)
    return y @ p['w_out'][l]


def dense_swiglu(h, wg, wu, wd):
    return (jax.nn.silu(h @ wg) * (h @ wu)) @ wd


def moe_swiglu(h, w_router, wg, wu, wd):
    B, S, D = h.shape
    T = B * S
    x2d = h.reshape(T, D)
    logits = (x2d @ w_router).astype(jnp.float32)
    top_vals, top_idx = lax.top_k(logits, TOP_K)
    probs = jax.nn.softmax(top_vals, axis=-1)
    A = T * TOP_K
    flat_e = top_idx.reshape(A).astype(jnp.int32)
    flat_tok = jnp.repeat(jnp.arange(T, dtype=jnp.int32), TOP_K)
    flat_w = probs.reshape(A)
    order = jnp.argsort(flat_e)
    sorted_e = flat_e[order]
    sorted_tok = flat_tok[order]
    sorted_w = flat_w[order]
    counts = jnp.bincount(flat_e, length=N_EXPERTS).astype(jnp.int32)
    padded = (counts + EXPERT_BLOCK - 1) // EXPERT_BLOCK * EXPERT_BLOCK
    padded_end = jnp.cumsum(padded)
    start = jnp.cumsum(counts) - counts
    start_pad = padded_end - padded
    dest = start_pad[sorted_e] + jnp.arange(A, dtype=jnp.int32) - start[sorted_e]
    n_blocks = (A + N_EXPERTS * (EXPERT_BLOCK - 1) + EXPERT_BLOCK - 1) // EXPERT_BLOCK
    P = n_blocks * EXPERT_BLOCK
    slot_tok = jnp.full((P,), T, jnp.int32).at[dest].set(sorted_tok)
    slot_w = jnp.zeros((P,), h.dtype).at[dest].set(sorted_w.astype(h.dtype))
    block_start = jnp.arange(n_blocks, dtype=jnp.int32) * EXPERT_BLOCK
    block_e = jnp.minimum(jnp.searchsorted(padded_end, block_start, side='right'), N_EXPERTS - 1)
    x_pad = jnp.concatenate([x2d, jnp.zeros((1, D), h.dtype)], axis=0)
    xs = x_pad[slot_tok].reshape(n_blocks, EXPERT_BLOCK, D)

    def expert_block(args):
        xb, e = args
        return (jax.nn.silu(xb @ wg[e]) * (xb @ wu[e])) @ wd[e]

    ys = lax.map(expert_block, (xs, block_e)).reshape(P, D)
    out = jnp.zeros((T + 1, D), h.dtype).at[slot_tok].add(ys * slot_w[:, None])
    return out[:T].reshape(B, S, D)


def encoder_trunk(x, c, p):
    for l in range(DEPTH):
        mod = jax.nn.silu(c) @ p['w_ada'][l] + p['b_ada'][l]
        sh1, sc1, g1, sh2, sc2, g2 = [m[:, None, :] for m in jnp.split(mod, 6, axis=-1)]
        hmix = rmsnorm(x, p['norm_mix_pre'][l]) * (1 + sc1) + sh1
        x = x + g1 * rmsnorm(mixer(hmix, p, l), p['norm_mix_post'][l])
        hff = rmsnorm(x, p['norm_ffn_pre'][l]) * (1 + sc2) + sh2
        j = l // 2
        if l % 2 == 0:
            f = dense_swiglu(hff, p['w_ff_gate'][j], p['w_ff_up'][j], p['w_ff_down'][j])
        else:
            f = moe_swiglu(hff, p['w_router'][j], p['w_exp_gate'][j], p['w_exp_up'][j], p['w_exp_down'][j])
        x = x + g2 * rmsnorm(f, p['norm_ffn_post'][l])
    return x


def _normal(k, shape, std):
    return std * jax.random.normal(k, shape, jnp.float32)


def setup_inputs(seed: int = 0) -> dict:
    key = jax.random.key(seed)
    ks = jax.random.split(key, 40)
    D = D_MODEL
    u = jax.random.uniform(ks[20], (DEPTH, 2, D_RNN), jnp.float32, minval=0.9, maxval=0.999)
    a0 = u ** (1.0 / RG_C)
    rg_lambda = jnp.log(a0) - jnp.log1p(-a0)
    return {
        'x_prompt': _normal(ks[0], (BATCH, SEQ, D), 1.0),
        'x_sample': _normal(ks[1], (DEC_BATCH, DEC_SEQ, D), 1.0),
        'c_prompt': _normal(ks[2], (BATCH, D), 1.0),
        'c_sample': _normal(ks[3], (DEC_BATCH, D), 1.0),
        'w_ada': _normal(ks[4], (DEPTH, D, 6 * D), 0.5 * D ** -0.5),
        'b_ada': _normal(ks[5], (DEPTH, 6 * D), 0.02),
        'norm_mix_pre': 1.0 + _normal(ks[6], (DEPTH, D), 0.05),
        'norm_mix_post': 1.0 + _normal(ks[7], (DEPTH, D), 0.05),
        'norm_ffn_pre': 1.0 + _normal(ks[8], (DEPTH, D), 0.05),
        'norm_ffn_post': 1.0 + _normal(ks[9], (DEPTH, D), 0.05),
        'w_in': _normal(ks[10], (DEPTH, D, IN_COLS), D ** -0.5),
        'q_norm': 1.0 + _normal(ks[11], (DEPTH, Q_LORA_RANK), 0.05),
        'w_q_up': _normal(ks[12], (DEPTH, Q_LORA_RANK, N_ATT_HEADS * (QK_NOPE_DIM + QK_ROPE_DIM)), Q_LORA_RANK ** -0.5),
        'kv_norm': 1.0 + _normal(ks[13], (DEPTH, KV_LORA_RANK), 0.05),
        'w_kv_up': _normal(ks[14], (DEPTH, KV_LORA_RANK, N_ATT_HEADS * (QK_NOPE_DIM + V_HEAD_DIM)), KV_LORA_RANK ** -0.5),
        'conv_w': _normal(ks[15], (DEPTH, CONV_WIDTH, D_RNN), CONV_WIDTH ** -0.5),
        'conv_b': _normal(ks[16], (DEPTH, D_RNN), 0.02),
        'w_rg_a': _normal(ks[17], (DEPTH, 2, RNN_BLOCKS, RNN_BLOCK_W, RNN_BLOCK_W), RNN_BLOCK_W ** -0.5),
        'b_rg_a': _normal(ks[18], (DEPTH, 2, D_RNN), 0.02),
        'w_rg_i': _normal(ks[19], (DEPTH, 2, RNN_BLOCKS, RNN_BLOCK_W, RNN_BLOCK_W), RNN_BLOCK_W ** -0.5),
        'b_rg_i': _normal(ks[21], (DEPTH, 2, D_RNN), 0.02),
        'rg_lambda': rg_lambda,
        'attn_out_norm': 1.0 + _normal(ks[22], (DEPTH, D_ATT), 0.05),
        'rnn_out_norm': 1.0 + _normal(ks[23], (DEPTH, D_RNN), 0.05),
        'w_out': _normal(ks[24], (DEPTH, D_MIX, D), D_MIX ** -0.5),
        'w_ff_gate': _normal(ks[25], (N_DENSE, D, D_FF), D ** -0.5),
        'w_ff_up': _normal(ks[26], (N_DENSE, D, D_FF), D ** -0.5),
        'w_ff_down': _normal(ks[27], (N_DENSE, D_FF, D), D_FF ** -0.5),
        'w_router': _normal(ks[28], (N_MOE, D, N_EXPERTS), D ** -0.5),
        'w_exp_gate': _normal(ks[29], (N_MOE, N_EXPERTS, D, D_FF), D ** -0.5),
        'w_exp_up': _normal(ks[30], (N_MOE, N_EXPERTS, D, D_FF), D ** -0.5),
        'w_exp_down': _normal(ks[31], (N_MOE, N_EXPERTS, D_FF, D), D_FF ** -0.5),
    }


def reference(x_prompt, x_sample, c_prompt, c_sample, w_ada, b_ada, norm_mix_pre, norm_mix_post,
              norm_ffn_pre, norm_ffn_post, w_in, q_norm, w_q_up, kv_norm, w_kv_up, conv_w, conv_b,
              w_rg_a, b_rg_a, w_rg_i, b_rg_i, rg_lambda, attn_out_norm, rnn_out_norm, w_out,
              w_ff_gate, w_ff_up, w_ff_down, w_router, w_exp_gate, w_exp_up, w_exp_down):
    p = dict(w_ada=w_ada, b_ada=b_ada, norm_mix_pre=norm_mix_pre, norm_mix_post=norm_mix_post,
             norm_ffn_pre=norm_ffn_pre, norm_ffn_post=norm_ffn_post, w_in=w_in, q_norm=q_norm,
             w_q_up=w_q_up, kv_norm=kv_norm, w_kv_up=w_kv_up, conv_w=conv_w, conv_b=conv_b,
             w_rg_a=w_rg_a, b_rg_a=b_rg_a, w_rg_i=w_rg_i, b_rg_i=b_rg_i, rg_lambda=rg_lambda,
             attn_out_norm=attn_out_norm, rnn_out_norm=rnn_out_norm, w_out=w_out,
             w_ff_gate=w_ff_gate, w_ff_up=w_ff_up, w_ff_down=w_ff_down, w_router=w_router,
             w_exp_gate=w_exp_gate, w_exp_up=w_exp_up, w_exp_down=w_exp_down)
    y_prompt = encoder_trunk(x_prompt, c_prompt, p)
    y_sample = encoder_trunk(x_sample, c_sample, p)
    return (y_prompt, y_sample)
```

```python
import functools

import numpy as np
import jax
import jax.numpy as jnp
from jax import lax
from jax.experimental import pallas as pl
from jax.experimental.pallas import tpu as pltpu

F32 = jnp.float32
BF16 = jnp.bfloat16

EPS = 1e-6
QK_NOPE_DIM = 128
QK_ROPE_DIM = 64
V_HEAD_DIM = 128
ROPE_THETA = 10000.0
RG_C = 8.0
TOP_K = 2
LANES = 128
VMEM_LIMIT = 56 * 1024 * 1024
MOE_ROWS = 1024
MOE_SUB = 512


def _cparams(*sem):
    return pltpu.CompilerParams(dimension_semantics=sem, vmem_limit_bytes=VMEM_LIMIT)


def _sigmoid(x):
    return 1.0 / (1.0 + jnp.exp(-x))


def _rms(x, g):
    return x * lax.rsqrt(jnp.mean(x * x, axis=-1, keepdims=True) + EPS) * g


def _rope_groups(acc, cs):
    m, n = acc.shape
    reps = n // LANES
    t = acc * (jnp.tile(cs, (1, reps)) if reps > 1 else cs)
    low = lax.broadcasted_iota(jnp.int32, (m, n), 1) % LANES < QK_ROPE_DIM
    outs = []
    for r in range(reps):
        tr = t[:, r * LANES:(r + 1) * LANES]
        outs.append(tr + pltpu.roll(tr, QK_ROPE_DIM, 1))
    y = outs[0] if reps == 1 else jnp.concatenate(outs, axis=1)
    return jnp.where(low, y, 0.0)


def _fk_kernel(pos_ref, *refs, n_in, prologue, has_bias, epilogue):
    del pos_ref
    refs = list(refs)
    xs = [refs.pop(0) for _ in range(n_in)]
    gs = [refs.pop(0) for _ in range(n_in)] if prologue == "rms" else []
    ws = [refs.pop(0) for _ in range(n_in)]
    bias = refs.pop(0) if has_bias else None
    cs = refs.pop(0) if epilogue is not None else None
    out = refs.pop(0)
    xbs = refs

    if prologue is not None:
        @pl.when(pl.program_id(1) == 0)
        def _():
            for i in range(n_in):
                x = xs[i][...].astype(F32)
                if prologue == "rms":
                    x = _rms(x, gs[i][...])
                else:
                    x = x * _sigmoid(x)
                xbs[i][...] = x.astype(BF16)
        lhs = [xb[...] for xb in xbs]
    else:
        lhs = [x[...] for x in xs]

    acc = None
    for i in range(n_in):
        part = jnp.dot(lhs[i], ws[i][...].astype(BF16), preferred_element_type=F32)
        acc = part if acc is None else acc + part
    if has_bias:
        acc = acc + bias[...]
    if epilogue == "rope":
        out[...] = _rope_groups(acc, cs[...]).astype(out.dtype)
    elif epilogue == "rope_last":
        last = pl.program_id(1) == pl.num_programs(1) - 1

        @pl.when(last)
        def _():
            out[...] = _rope_groups(acc, cs[...]).astype(out.dtype)

        @pl.when(jnp.logical_not(last))
        def _():
            out[...] = acc.astype(out.dtype)
    else:
        out[...] = acc.astype(out.dtype)


def fk_matmul(name, xs, ws, *, tm, tn, out_dtype, x_cols=None, gs=None, prologue=None, bias=None,
              w_lead=None, w_rows=None, epilogue=None, cs=None, tile_pos=None):
    n_in = len(xs)
    M = xs[0].shape[0]
    N = ws[0].shape[-1]
    assert M % tm == 0 and N % tn == 0
    grid = (M // tm, N // tn)
    if tile_pos is None:
        tile_pos = jnp.zeros((grid[0],), jnp.int32)
    if x_cols is None:
        x_cols = [(x.shape[1], 0) for x in xs]

    in_specs, args = [], []
    for x, (kw, cb) in zip(xs, x_cols):
        in_specs.append(pl.BlockSpec((tm, kw), lambda i, j, p, cb=cb: (i, cb)))
        args.append(x)
    if prologue == "rms":
        for g in gs:
            in_specs.append(pl.BlockSpec((1, g.shape[1]), lambda i, j, p: (0, 0)))
            args.append(g)
    for idx, ((kw, _), w) in enumerate(zip(x_cols, ws)):
        if w.ndim == 3:
            rb = 0 if w_rows is None else w_rows[idx]
            in_specs.append(pl.BlockSpec((None, kw, tn), lambda i, j, p, rb=rb: (w_lead, rb, j)))
        else:
            assert w.shape[0] == kw
            in_specs.append(pl.BlockSpec((kw, tn), lambda i, j, p: (0, j)))
        args.append(w)
    if bias is not None:
        in_specs.append(pl.BlockSpec((1, tn), lambda i, j, p: (0, j)))
        args.append(bias)
    if epilogue is not None:
        in_specs.append(pl.BlockSpec((tm, LANES), lambda i, j, p: (p[i], 0)))
        args.append(cs)

    scratch = []
    if prologue is not None:
        scratch = [pltpu.VMEM((tm, kw), BF16) for kw, _ in x_cols]

    kern = functools.partial(_fk_kernel, n_in=n_in, prologue=prologue, has_bias=bias is not None,
                             epilogue=epilogue)
    return pl.pallas_call(
        kern,
        grid_spec=pltpu.PrefetchScalarGridSpec(
            num_scalar_prefetch=1, grid=grid, in_specs=in_specs,
            out_specs=pl.BlockSpec((tm, tn), lambda i, j, p: (i, j)),
            scratch_shapes=scratch),
        out_shape=jax.ShapeDtypeStruct((M, N), out_dtype),
        compiler_params=_cparams("parallel", "arbitrary"),
        name=name,
    )(tile_pos, *args)


def _mod_row(mod_ref, chunk, s):
    return mod_ref[chunk, pl.ds(s, 1), :]


def _prenorm_kernel(seq_ref, x_ref, g_ref, mod_ref, h_ref, *, sc, sh):
    s = seq_ref[pl.program_id(0)]
    y = _rms(x_ref[...], g_ref[...])
    h_ref[...] = (y * (1.0 + _mod_row(mod_ref, sc, s)) + _mod_row(mod_ref, sh, s)).astype(h_ref.dtype)


def prenorm(x, g, mod, seq_of_tile, *, tm, sc, sh):
    T, D = x.shape
    return pl.pallas_call(
        functools.partial(_prenorm_kernel, sc=sc, sh=sh),
        grid_spec=pltpu.PrefetchScalarGridSpec(
            num_scalar_prefetch=1, grid=(T // tm,),
            in_specs=[pl.BlockSpec((tm, D), lambda i, s: (i, 0)),
                      pl.BlockSpec((1, D), lambda i, s: (0, 0)),
                      pl.BlockSpec(mod.shape, lambda i, s: (0, 0, 0))],
            out_specs=pl.BlockSpec((tm, D), lambda i, s: (i, 0))),
        out_shape=jax.ShapeDtypeStruct((T, D), BF16),
        compiler_params=_cparams("parallel"),
        name="prenorm",
    )(seq_of_tile, x, g, mod)


def _post_kernel(seq_ref, x_ref, y_ref, gpost_ref, mod_ref, *rest, gate, sc, sh, has_next, has_router):
    rest = list(rest)
    gpre_ref = rest.pop(0) if has_next else None
    wr_ref = rest.pop(0) if has_router else None
    xo_ref = rest.pop(0)
    h_ref = rest.pop(0) if has_next else None
    lg_ref = rest.pop(0) if has_router else None
    s = seq_ref[pl.program_id(0)]
    xn = x_ref[...] + _mod_row(mod_ref, gate, s) * _rms(y_ref[...], gpost_ref[...])
    xo_ref[...] = xn
    if has_next:
        h = _rms(xn, gpre_ref[...]) * (1.0 + _mod_row(mod_ref, sc, s)) + _mod_row(mod_ref, sh, s)
        h_ref[...] = h.astype(h_ref.dtype)
        if has_router:
            lg_ref[...] = jnp.dot(h, wr_ref[...], preferred_element_type=F32,
                                  precision=lax.Precision.HIGHEST)


def post_residual(x, y, gpost, mod, seq_of_tile, *, tm, gate, gpre=None, sc=None, sh=None, w_router=None):
    T, D = x.shape
    has_next = gpre is not None
    has_router = w_router is not None
    row = pl.BlockSpec((tm, D), lambda i, s: (i, 0))
    vec = pl.BlockSpec((1, D), lambda i, s: (0, 0))
    in_specs = [row, row, vec, pl.BlockSpec(mod.shape, lambda i, s: (0, 0, 0))]
    args = [x, y, gpost, mod]
    out_shape = [jax.ShapeDtypeStruct((T, D), F32)]
    out_specs = [row]
    if has_next:
        in_specs.append(vec)
        args.append(gpre)
        out_shape.append(jax.ShapeDtypeStruct((T, D), BF16))
        out_specs.append(row)
    if has_router:
        in_specs.append(pl.BlockSpec(w_router.shape, lambda i, s: (0, 0)))
        args.append(w_router)
        out_shape.append(jax.ShapeDtypeStruct((T, w_router.shape[1]), F32))
        out_specs.append(pl.BlockSpec((tm, w_router.shape[1]), lambda i, s: (i, 0)))
    return pl.pallas_call(
        functools.partial(_post_kernel, gate=gate, sc=sc, sh=sh, has_next=has_next, has_router=has_router),
        grid_spec=pltpu.PrefetchScalarGridSpec(
            num_scalar_prefetch=1, grid=(T // tm,), in_specs=in_specs, out_specs=out_specs),
        out_shape=out_shape,
        compiler_params=_cparams("parallel"),
        name="post_residual",
    )(seq_of_tile, *args)


def _attn_kernel(qn_ref, qr_ref, kn_ref, kr_ref, v_ref, o_ref, *, scale):
    q = jnp.concatenate([qn_ref[...], qr_ref[...]], axis=1)
    k = jnp.concatenate([kn_ref[...], kr_ref[...].astype(BF16)], axis=1)
    s = lax.dot_general(q, k, (((1,), (1,)), ((), ())), preferred_element_type=F32) * scale
    m = jnp.max(s, axis=1, keepdims=True)
    p = jnp.exp(s - m)
    l = jnp.sum(p, axis=1, keepdims=True)
    o = jnp.dot(p.astype(BF16), v_ref[...], preferred_element_type=F32)
    o_ref[...] = (o / l).astype(o_ref.dtype)


def attention(qn, qr, kv, kr, kr_blk, *, n_heads, batch, seq, row0, tq, scale, name):
    assert row0 % seq == 0 and seq % tq == 0
    b0 = row0 // seq
    nq = seq // tq
    q0 = row0 // tq
    H = n_heads
    return pl.pallas_call(
        functools.partial(_attn_kernel, scale=scale),
        grid=(batch, H, nq),
        in_specs=[
            pl.BlockSpec((tq, LANES), lambda b, h, i: (q0 + b * nq + i, h)),
            pl.BlockSpec((tq, LANES), lambda b, h, i: (q0 + b * nq + i, h)),
            pl.BlockSpec((seq, LANES), lambda b, h, i: (b0 + b, h)),
            pl.BlockSpec((seq, LANES), lambda b, h, i: (b0 + b, kr_blk)),
            pl.BlockSpec((seq, LANES), lambda b, h, i: (b0 + b, H + h)),
        ],
        out_specs=pl.BlockSpec((tq, LANES), lambda b, h, i: (b * nq + i, h)),
        out_shape=jax.ShapeDtypeStruct((batch * seq, H * V_HEAD_DIM), F32),
        compiler_params=_cparams("parallel", "parallel", "arbitrary"),
        name=name,
    )(qn, qr, kv, kr, kv)


def _rglru_kernel(rx_ref, rg_ref, rp_ref, w_ref, o_ref, xpad, a0, b0, a1, b1, *, tt):
    S, tc = rx_ref.shape
    nb = tc // LANES
    zeros8 = jnp.zeros((8, tc), F32)
    xpad[0:8, :] = zeros8
    xpad[8 + S:16 + S, :] = zeros8
    xpad[8:8 + S, :] = rx_ref[...]
    rp = rp_ref[...]
    cw = [rp[j:j + 1, :] for j in range(4)]
    cb = rp[4:5, :]
    b_a = (rp[5:6, :], rp[6:7, :])
    b_i = (rp[7:8, :], rp[8:9, :])
    sp = tuple(jnp.maximum(-lam, 0.0) + jnp.log1p(jnp.exp(-jnp.abs(lam))) for lam in (rp[9:10, :], rp[10:11, :]))
    ab = ((a0, b0), (a1, b1))

    def chunk(c, carry):
        c0 = pl.multiple_of(c * tt, tt)
        xw = xpad[pl.ds(c0, tt + 16), :]
        xc = (cw[0] * xw[6:6 + tt] + cw[1] * xw[7:7 + tt] + cw[2] * xw[8:8 + tt] + cw[3] * xw[9:9 + tt]) + cb
        for n in range(nb):
            sl = slice(n * LANES, (n + 1) * LANES)
            xcn = xc[:, sl]
            z = jnp.dot(xcn.astype(BF16), w_ref[n].astype(BF16), preferred_element_type=F32)
            for d in range(2):
                za = z[:, d * LANES:(d + 1) * LANES] + b_a[d][:, sl]
                zi = z[:, (2 + d) * LANES:(3 + d) * LANES] + b_i[d][:, sl]
                log_a = (-RG_C) * _sigmoid(za) * sp[d][:, sl]
                th = jnp.tanh(log_a)
                gain = jnp.sqrt(-2.0 * th / (1.0 - th))
                ab[d][0][pl.ds(c0, tt), sl] = jnp.exp(log_a)
                ab[d][1][pl.ds(c0, tt), sl] = gain * _sigmoid(zi) * xcn
        return carry

    lax.fori_loop(0, S // tt, chunk, 0)

    def scan(t8, hs):
        hf, hb = hs
        base = pl.multiple_of(t8 * 8, 8)
        for j in range(8):
            tf = base + j
            hf = a0[pl.ds(tf, 1), :] * hf + b0[pl.ds(tf, 1), :]
            b0[pl.ds(tf, 1), :] = hf
            tb = S - 1 - tf
            hb = a1[pl.ds(tb, 1), :] * hb + b1[pl.ds(tb, 1), :]
            b1[pl.ds(tb, 1), :] = hb
        return hf, hb

    h0 = jnp.zeros((1, tc), F32)
    lax.fori_loop(0, S // 8, scan, (h0, h0))

    def gate(c, carry):
        c0 = pl.multiple_of(c * tt, tt)
        g = rg_ref[pl.ds(c0, tt), :]
        gelu = 0.5 * g * (1.0 + jnp.tanh(0.7978845608028654 * (g + 0.044715 * (g * g * g))))
        o_ref[pl.ds(c0, tt), :] = gelu * (b0[pl.ds(c0, tt), :] + b1[pl.ds(c0, tt), :])
        return carry

    lax.fori_loop(0, S // tt, gate, 0)


def rglru(r, rp, w_cat, *, batch, seq, row0, tc, name):
    C = r.shape[1] // 2
    assert row0 % seq == 0 and C % tc == 0
    b0 = row0 // seq
    nct = C // tc
    tt = min(256, seq)
    return pl.pallas_call(
        functools.partial(_rglru_kernel, tt=tt),
        grid=(batch, nct),
        in_specs=[
            pl.BlockSpec((seq, tc), lambda b, c: (b0 + b, c)),
            pl.BlockSpec((seq, tc), lambda b, c: (b0 + b, nct + c)),
            pl.BlockSpec((16, tc), lambda b, c: (0, c)),
            pl.BlockSpec((tc // LANES, LANES, 4 * LANES), lambda b, c: (c, 0, 0)),
        ],
        out_specs=pl.BlockSpec((seq, tc), lambda b, c: (b, c)),
        out_shape=jax.ShapeDtypeStruct((batch * seq, C), F32),
        scratch_shapes=[pltpu.VMEM((seq + 16, tc), F32)] + [pltpu.VMEM((seq, tc), F32)] * 4,
        compiler_params=_cparams("parallel", "parallel"),
        name=name,
    )(r, r, rp, w_cat)


def _glu_kernel(be_ref, bv_ref, nr_ref, x_ref, wg_ref, wu_ref, o_ref, wgb, wub):
    del nr_ref
    i = pl.program_id(1)
    changed = jnp.logical_or(i == 0, be_ref[i] != be_ref[jnp.maximum(i - 1, 0)])

    @pl.when(changed)
    def _():
        wgb[...] = wg_ref[...].astype(BF16)
        wub[...] = wu_ref[...].astype(BF16)

    @pl.when(bv_ref[i] > 0)
    def _():
        x = x_ref[...]
        g = jnp.dot(x, wgb[...], preferred_element_type=F32)
        u = jnp.dot(x, wub[...], preferred_element_type=F32)
        o_ref[...] = (g * _sigmoid(g) * u).astype(o_ref.dtype)


def glu_up(x, wg, wu, block_e, block_valid, n_real, *, tm, tn):
    P, D = x.shape
    F = wg.shape[2]
    nb = P // tm

    def row(j, i, be, bv, nr):
        return jnp.minimum(i, nr[0] - 1)

    return pl.pallas_call(
        _glu_kernel,
        grid_spec=pltpu.PrefetchScalarGridSpec(
            num_scalar_prefetch=3, grid=(F // tn, nb),
            in_specs=[
                pl.BlockSpec((tm, D), lambda j, i, be, bv, nr: (row(j, i, be, bv, nr), 0)),
                pl.BlockSpec((None, D, tn), lambda j, i, be, bv, nr: (be[row(j, i, be, bv, nr)], 0, j)),
                pl.BlockSpec((None, D, tn), lambda j, i, be, bv, nr: (be[row(j, i, be, bv, nr)], 0, j)),
            ],
            out_specs=pl.BlockSpec((tm, tn), lambda j, i, be, bv, nr: (row(j, i, be, bv, nr), j)),
            scratch_shapes=[pltpu.VMEM((D, tn), BF16), pltpu.VMEM((D, tn), BF16)]),
        out_shape=jax.ShapeDtypeStruct((P, F), BF16),
        compiler_params=_cparams("arbitrary", "arbitrary"),
        name="glu_up",
    )(block_e, block_valid, n_real, x, wg, wu)


def _down_kernel(be_ref, bv_ref, nr_ref, h_ref, w_ref, o_ref, *, half):
    del be_ref, nr_ref
    valid = bv_ref[pl.program_id(1)]

    @pl.when(jnp.logical_and(pl.program_id(2) == 0, valid > 0))
    def _():
        o_ref[...] = jnp.zeros_like(o_ref)

    @pl.when(valid > half)
    def _():
        o_ref[...] += jnp.dot(h_ref[...], w_ref[...].astype(BF16), preferred_element_type=F32)

    @pl.when(jnp.logical_and(valid > 0, valid <= half))
    def _():
        o_ref[:half, :] += jnp.dot(h_ref[:half, :], w_ref[...].astype(BF16), preferred_element_type=F32)


def glu_down(h, wd, block_e, block_valid, n_real, *, tm, tn, tk):
    P, F = h.shape
    D = wd.shape[2]
    nb = P // tm

    def row(i, nr):
        return jnp.minimum(i, nr[0] - 1)

    return pl.pallas_call(
        functools.partial(_down_kernel, half=tm // 2),
        grid_spec=pltpu.PrefetchScalarGridSpec(
            num_scalar_prefetch=3, grid=(D // tn, nb, F // tk),
            in_specs=[
                pl.BlockSpec((tm, tk), lambda n, i, k, be, bv, nr: (row(i, nr), k)),
                pl.BlockSpec((None, tk, tn), lambda n, i, k, be, bv, nr: (be[row(i, nr)], k, n)),
            ],
            out_specs=pl.BlockSpec((tm, tn), lambda n, i, k, be, bv, nr: (row(i, nr), n))),
        out_shape=jax.ShapeDtypeStruct((P, D), F32),
        compiler_params=_cparams("arbitrary", "arbitrary", "arbitrary"),
        name="glu_down",
    )(block_e, block_valid, n_real, h, wd)


def _rot_half_cols(w):
    half = QK_ROPE_DIM // 2
    return jnp.concatenate([-w[..., half:], w[..., :half]], axis=-1)


def _rope_table(S):
    inv = 1.0 / (ROPE_THETA ** (jnp.arange(0, QK_ROPE_DIM, 2, dtype=F32) / QK_ROPE_DIM))
    ang = jnp.arange(S, dtype=F32)[:, None] * inv[None, :]
    c, s = jnp.cos(ang), jnp.sin(ang)
    return jnp.concatenate([c, c, s, s], axis=1)


def _tile_maps(seqs, tm):
    seq_of_tile, tile_pos = [], []
    for sid, (row0, length) in enumerate(seqs):
        assert row0 % tm == 0 and length % tm == 0
        for t in range(length // tm):
            seq_of_tile.append(sid)
            tile_pos.append(t)
    return jnp.asarray(np.array(seq_of_tile, np.int32)), jnp.asarray(np.array(tile_pos, np.int32))


def _moe_plan(logits, n_experts, rows):
    T = logits.shape[0]
    A = T * TOP_K
    top_vals, top_idx = lax.top_k(logits, TOP_K)
    probs = jax.nn.softmax(top_vals, axis=-1)
    flat_e = top_idx.reshape(A).astype(jnp.int32)
    flat_tok = jnp.repeat(jnp.arange(T, dtype=jnp.int32), TOP_K)
    order = jnp.argsort(flat_e)
    sorted_e = flat_e[order]
    counts = jnp.bincount(flat_e, length=n_experts).astype(jnp.int32)
    padded = (counts + rows - 1) // rows * rows
    padded_end = jnp.cumsum(padded)
    start = jnp.cumsum(counts) - counts
    start_pad = padded_end - padded
    dest = start_pad[sorted_e] + jnp.arange(A, dtype=jnp.int32) - start[sorted_e]
    n_blocks = (A + n_experts * (rows - 1) + rows - 1) // rows
    P = n_blocks * rows
    slot_tok = jnp.zeros((P,), jnp.int32).at[dest].set(flat_tok[order])
    pos = jnp.zeros((A,), jnp.int32).at[order].set(dest).reshape(T, TOP_K)
    block_start = jnp.arange(n_blocks, dtype=jnp.int32) * rows
    block_e = jnp.minimum(jnp.searchsorted(padded_end, block_start, side="right"), n_experts - 1).astype(jnp.int32)
    group_end = start_pad + counts
    block_valid = jnp.clip(group_end[block_e] - block_start, 0, rows)
    block_valid = jnp.where(block_start < padded_end[-1], block_valid, 0).astype(jnp.int32)
    n_real = (padded_end[-1] // rows).astype(jnp.int32)
    return slot_tok, pos, probs, block_e, block_valid, n_real


def _sub_blocks(block_e, block_valid, n_real, rows, sub):
    r = rows // sub
    be = jnp.repeat(block_e, r)
    off = jnp.tile(jnp.arange(r, dtype=jnp.int32) * sub, block_e.shape[0])
    bv = jnp.clip(jnp.repeat(block_valid, r) - off, 0, sub).astype(jnp.int32)
    return be, bv, (n_real * r).reshape(1).astype(jnp.int32)


def kernel(x_prompt, x_sample, c_prompt, c_sample, w_ada, b_ada, norm_mix_pre, norm_mix_post, norm_ffn_pre, norm_ffn_post, w_in, q_norm, w_q_up, kv_norm, w_kv_up, conv_w, conv_b, w_rg_a, b_rg_a, w_rg_i, b_rg_i, rg_lambda, attn_out_norm, rnn_out_norm, w_out, w_ff_gate, w_ff_up, w_ff_down, w_router, w_exp_gate, w_exp_up, w_exp_down):
    B, S, D = x_prompt.shape
    DB, DS, _ = x_sample.shape
    depth = w_in.shape[0]
    T0, T1 = B * S, DB * DS
    T = T0 + T1
    q_rank = q_norm.shape[1]
    kv_rank = kv_norm.shape[1]
    H = w_q_up.shape[2] // (QK_NOPE_DIM + QK_ROPE_DIM)
    C = conv_w.shape[2]
    n_rnn_blocks = w_rg_a.shape[2]
    assert w_rg_a.shape[3] == LANES and C == n_rnn_blocks * LANES
    F_ff = w_ff_gate.shape[2]
    E = w_router.shape[2]
    scale = (QK_NOPE_DIM + QK_ROPE_DIM) ** -0.5

    seqs = [(b * S, S) for b in range(B)] + [(T0 + b * DS, DS) for b in range(DB)]
    s_min = min(S, DS)
    tm = min(1024, s_min)
    te = min(256, s_min)
    _, pos_tm = _tile_maps(seqs, tm)
    seq_te, _ = _tile_maps(seqs, te)
    cs_table = _rope_table(max(S, DS))

    x = jnp.concatenate([x_prompt.reshape(T0, D), x_sample.reshape(T1, D)], axis=0)
    n_seq = B + DB
    c_all = jnp.concatenate([c_prompt, c_sample, jnp.zeros((-n_seq % 16, D), F32)], axis=0)

    def vec(p):
        return p.reshape(1, -1)

    mods = []
    for l in range(depth):
        m = fk_matmul("ada", [c_all], [w_ada], w_lead=l, prologue="silu", bias=vec(b_ada[l]),
                      tm=c_all.shape[0], tn=min(1024, 6 * D), out_dtype=F32)
        mods.append(m.reshape(c_all.shape[0], 6, D).transpose(1, 0, 2))
    mod = jnp.concatenate(mods, axis=0)

    tq = min(512, s_min)
    rows = min(MOE_ROWS, s_min)
    sub = min(MOE_SUB, rows)
    tn_ff = min(512, F_ff)
    tk_ff = min(1024, F_ff)
    tn_dn = min(2048, D)

    hmix = prenorm(x, vec(norm_mix_pre[0]), mod, seq_te, tm=te, sc=1, sh=0)

    for l in range(depth):
        m0 = 6 * l
        wl = w_in[l]
        n_lat = q_rank + kv_rank
        w_kr = wl[:, n_lat:n_lat + QK_ROPE_DIM]
        n_qkv = n_lat + LANES
        tn_qkv = next(t for t in (512, 256, 128) if n_lat % t == 0)
        assert q_rank % kv_rank == 0
        w_qkv = jnp.concatenate([wl[:, :n_lat + QK_ROPE_DIM], _rot_half_cols(w_kr),
                                 jnp.zeros((D, -n_qkv % tn_qkv), F32)], axis=1)
        w_r = wl[:, n_lat + QK_ROPE_DIM:]
        qkv = fk_matmul("qkv_proj", [hmix], [w_qkv], tm=tm, tn=tn_qkv, out_dtype=F32, epilogue="rope_last",
                        cs=cs_table, tile_pos=pos_tm)
        r = fk_matmul("r_proj", [hmix], [w_r], tm=tm, tn=min(512, 2 * C), out_dtype=F32)

        wq = w_q_up[l].reshape(q_rank, H, QK_NOPE_DIM + QK_ROPE_DIM)
        w_qn = wq[:, :, :QK_NOPE_DIM].reshape(q_rank, H * QK_NOPE_DIM)
        wq_r = wq[:, :, QK_NOPE_DIM:]
        w_qr = jnp.concatenate([wq_r, _rot_half_cols(wq_r)], axis=-1).reshape(q_rank, H * LANES)
        wkv = w_kv_up[l].reshape(kv_rank, H, QK_NOPE_DIM + V_HEAD_DIM)
        w_kv = jnp.concatenate([wkv[:, :, :QK_NOPE_DIM].reshape(kv_rank, H * QK_NOPE_DIM),
                                wkv[:, :, QK_NOPE_DIM:].reshape(kv_rank, H * V_HEAD_DIM)], axis=1)
        qg = vec(q_norm[l])
        q_cols = [(q_rank, 0)]
        kv_cols = [(kv_rank, q_rank // kv_rank)]
        qn = fk_matmul("q_nope", [qkv], [w_qn], x_cols=q_cols, gs=[qg], prologue="rms", tm=tm,
                       tn=min(1024, w_qn.shape[1]), out_dtype=BF16)
        qr = fk_matmul("q_rope", [qkv], [w_qr], x_cols=q_cols, gs=[qg], prologue="rms", tm=tm,
                       tn=min(1024, w_qr.shape[1]), out_dtype=BF16, epilogue="rope",
                       cs=cs_table, tile_pos=pos_tm)
        kv = fk_matmul("kv_up", [qkv], [w_kv], x_cols=kv_cols, gs=[vec(kv_norm[l])], prologue="rms", tm=tm,
                       tn=min(1024, w_kv.shape[1]), out_dtype=BF16)
        kr_blk = n_lat // LANES

        y_att = jnp.concatenate([
            attention(qn, qr, kv, qkv, kr_blk, n_heads=H, batch=B, seq=S, row0=0, tq=min(tq, S), scale=scale,
                      name="attn_prompt"),
            attention(qn, qr, kv, qkv, kr_blk, n_heads=H, batch=DB, seq=DS, row0=T0, tq=min(tq, DS), scale=scale,
                      name="attn_sample"),
        ], axis=0)

        rp = jnp.concatenate([conv_w[l], conv_b[l][None], b_rg_a[l], b_rg_i[l], rg_lambda[l],
                              jnp.zeros((5, C), F32)], axis=0)
        w_cat = jnp.concatenate([w_rg_a[l, 0], w_rg_a[l, 1], w_rg_i[l, 0], w_rg_i[l, 1]], axis=-1)
        tc = min(256, C)
        y_rnn = jnp.concatenate([
            rglru(r, rp, w_cat, batch=B, seq=S, row0=0, tc=tc, name="rglru_prompt"),
            rglru(r, rp, w_cat, batch=DB, seq=DS, row0=T0, tc=tc, name="rglru_sample"),
        ], axis=0)

        assert w_out.shape[1] == 2 * y_att.shape[1] and y_rnn.shape[1] == y_att.shape[1]
        y_mix = fk_matmul("out_proj", [y_att, y_rnn], [w_out, w_out], w_lead=l, w_rows=(0, 1),
                          gs=[vec(attn_out_norm[l]), vec(rnn_out_norm[l])], prologue="rms",
                          tm=min(tm, 512), tn=min(512, D), out_dtype=F32)

        j = l // 2
        if l % 2 == 0:
            x, hff = post_residual(x, y_mix, vec(norm_mix_post[l]), mod, seq_te, tm=te, gate=m0 + 2,
                                   gpre=vec(norm_ffn_pre[l]), sc=m0 + 4, sh=m0 + 3)
            nb = T // rows
            be = jnp.full((nb,), j, jnp.int32)
            bv = jnp.full((nb,), rows, jnp.int32)
            nr = jnp.full((), nb, jnp.int32)
            sbe, sbv, snr = _sub_blocks(be, bv, nr, rows, sub)
            hidden = glu_up(hff, w_ff_gate, w_ff_up, sbe, sbv, snr, tm=sub, tn=tn_ff)
            f = glu_down(hidden, w_ff_down, be, bv, nr.reshape(1), tm=rows, tn=tn_dn, tk=tk_ff)
        else:
            w_r_pad = jnp.concatenate([w_router[j], jnp.zeros((D, LANES - E), F32)], axis=1)
            x, hff, logits = post_residual(x, y_mix, vec(norm_mix_post[l]), mod, seq_te, tm=te, gate=m0 + 2,
                                           gpre=vec(norm_ffn_pre[l]), sc=m0 + 4, sh=m0 + 3, w_router=w_r_pad)
            slot_tok, pos, probs, be, bv, nr = _moe_plan(logits[:, :E], E, rows)
            be = be + j * E
            sbe, sbv, snr = _sub_blocks(be, bv, nr, rows, sub)
            xs = jnp.take(hff, slot_tok, axis=0)
            n_moe = w_exp_gate.shape[0]
            hidden = glu_up(xs, w_exp_gate.reshape(n_moe * E, D, F_ff), w_exp_up.reshape(n_moe * E, D, F_ff),
                            sbe, sbv, snr, tm=sub, tn=tn_ff)
            ys = glu_down(hidden, w_exp_down.reshape(n_moe * E, F_ff, D), be, bv, nr.reshape(1),
                          tm=rows, tn=tn_dn, tk=tk_ff)
            f = (jnp.take(ys, pos[:, 0], axis=0) * probs[:, 0:1]
                 + jnp.take(ys, pos[:, 1], axis=0) * probs[:, 1:2])

        if l + 1 < depth:
            x, hmix = post_residual(x, f, vec(norm_ffn_post[l]), mod, seq_te, tm=te, gate=m0 + 5,
                                    gpre=vec(norm_mix_pre[l + 1]), sc=m0 + 7, sh=m0 + 6)
        else:
            (x,) = post_residual(x, f, vec(norm_ffn_post[l]), mod, seq_te, tm=te, gate=m0 + 5)

    y_prompt = x[:T0].reshape(B, S, D)
    y_sample = x[T0:].reshape(DB, DS, D)
    return (y_prompt, y_sample)
```

```python
import functools

import numpy as np
import jax
import jax.numpy as jnp
from jax import lax
from jax.experimental import pallas as pl
from jax.experimental.pallas import tpu as pltpu

F32 = jnp.float32
BF16 = jnp.bfloat16

EPS = 1e-6
QK_NOPE_DIM = 128
QK_ROPE_DIM = 64
V_HEAD_DIM = 128
ROPE_THETA = 10000.0
RG_C = 8.0
LOG2_E = 1.4426950408889634
TOP_K = 2
LANES = 128
VMEM_LIMIT = 56 * 1024 * 1024
MOE_ROWS = 1024
MOE_SUB = 512
MOE_PART = 256


def _cparams(*sem):
    return pltpu.CompilerParams(dimension_semantics=sem, vmem_limit_bytes=VMEM_LIMIT)


def _sigmoid(x):
    return 1.0 / (1.0 + jnp.exp(-x))


def _rms(x, g):
    return x * lax.rsqrt(jnp.mean(x * x, axis=-1, keepdims=True) + EPS) * g


def _rope_groups(acc, cs):
    m, n = acc.shape
    reps = n // LANES
    t = acc * (jnp.tile(cs, (1, reps)) if reps > 1 else cs)
    low = lax.broadcasted_iota(jnp.int32, (m, n), 1) % LANES < QK_ROPE_DIM
    outs = []
    for r in range(reps):
        tr = t[:, r * LANES:(r + 1) * LANES]
        outs.append(tr + pltpu.roll(tr, QK_ROPE_DIM, 1))
    y = outs[0] if reps == 1 else jnp.concatenate(outs, axis=1)
    return jnp.where(low, y, 0.0)


def _fk_kernel(pos_ref, *refs, n_in, prologue, has_bias, epilogue):
    del pos_ref
    refs = list(refs)
    xs = [refs.pop(0) for _ in range(n_in)]
    gs = [refs.pop(0) for _ in range(n_in)] if prologue == "rms" else []
    ws = [refs.pop(0) for _ in range(n_in)]
    bias = refs.pop(0) if has_bias else None
    cs = refs.pop(0) if epilogue is not None else None
    out = refs.pop(0)
    xbs = refs

    if prologue is not None:
        @pl.when(pl.program_id(1) == 0)
        def _():
            for i in range(n_in):
                x = xs[i][...].astype(F32)
                if prologue == "rms":
                    x = _rms(x, gs[i][...])
                else:
                    x = x * _sigmoid(x)
                xbs[i][...] = x.astype(BF16)
        lhs = [xb[...] for xb in xbs]
    else:
        lhs = [x[...] for x in xs]

    acc = None
    for i in range(n_in):
        part = jnp.dot(lhs[i], ws[i][...].astype(BF16), preferred_element_type=F32)
        acc = part if acc is None else acc + part
    if has_bias:
        acc = acc + bias[...]
    if epilogue == "rope":
        out[...] = _rope_groups(acc, cs[...]).astype(out.dtype)
    else:
        out[...] = acc.astype(out.dtype)


def fk_matmul(name, xs, ws, *, tm, tn, out_dtype, x_cols=None, gs=None, prologue=None, bias=None,
              w_lead=None, w_rows=None, n_cols=None, epilogue=None, cs=None, tile_pos=None):
    n_in = len(xs)
    M = xs[0].shape[0]
    N = ws[0].shape[-1] if n_cols is None else n_cols
    assert M % tm == 0 and N % tn == 0
    grid = (M // tm, N // tn)
    if tile_pos is None:
        tile_pos = jnp.zeros((grid[0],), jnp.int32)
    if x_cols is None:
        x_cols = [(x.shape[1], 0) for x in xs]

    in_specs, args = [], []
    for x, (kw, cb) in zip(xs, x_cols):
        in_specs.append(pl.BlockSpec((tm, kw), lambda i, j, p, cb=cb: (i, cb)))
        args.append(x)
    if prologue == "rms":
        for g in gs:
            in_specs.append(pl.BlockSpec((1, g.shape[1]), lambda i, j, p: (0, 0)))
            args.append(g)
    for idx, ((kw, _), w) in enumerate(zip(x_cols, ws)):
        if w.ndim == 3:
            rb = 0 if w_rows is None else w_rows[idx]
            in_specs.append(pl.BlockSpec((None, kw, tn), lambda i, j, p, rb=rb: (w_lead, rb, j)))
        else:
            assert w.shape[0] == kw
            in_specs.append(pl.BlockSpec((kw, tn), lambda i, j, p: (0, j)))
        args.append(w)
    if bias is not None:
        in_specs.append(pl.BlockSpec((1, tn), lambda i, j, p: (0, j)))
        args.append(bias)
    if epilogue is not None:
        in_specs.append(pl.BlockSpec((tm, LANES), lambda i, j, p: (p[i], 0)))
        args.append(cs)

    scratch = []
    if prologue is not None:
        scratch = [pltpu.VMEM((tm, kw), BF16) for kw, _ in x_cols]

    kern = functools.partial(_fk_kernel, n_in=n_in, prologue=prologue, has_bias=bias is not None,
                             epilogue=epilogue)
    return pl.pallas_call(
        kern,
        grid_spec=pltpu.PrefetchScalarGridSpec(
            num_scalar_prefetch=1, grid=grid, in_specs=in_specs,
            out_specs=pl.BlockSpec((tm, tn), lambda i, j, p: (i, j)),
            scratch_shapes=scratch),
        out_shape=jax.ShapeDtypeStruct((M, N), out_dtype),
        compiler_params=_cparams("parallel", "arbitrary"),
        name=name,
    )(tile_pos, *args)


def _mod_row(mod_ref, chunk, s):
    return mod_ref[chunk, pl.ds(s, 1), :]


def _prenorm_kernel(seq_ref, x_ref, g_ref, mod_ref, h_ref, *, sc, sh):
    s = seq_ref[pl.program_id(0)]
    y = _rms(x_ref[...], g_ref[...])
    h_ref[...] = (y * (1.0 + _mod_row(mod_ref, sc, s)) + _mod_row(mod_ref, sh, s)).astype(h_ref.dtype)


def prenorm(x, g, mod, seq_of_tile, *, tm, sc, sh):
    T, D = x.shape
    return pl.pallas_call(
        functools.partial(_prenorm_kernel, sc=sc, sh=sh),
        grid_spec=pltpu.PrefetchScalarGridSpec(
            num_scalar_prefetch=1, grid=(T // tm,),
            in_specs=[pl.BlockSpec((tm, D), lambda i, s: (i, 0)),
                      pl.BlockSpec((1, D), lambda i, s: (0, 0)),
                      pl.BlockSpec(mod.shape, lambda i, s: (0, 0, 0))],
            out_specs=pl.BlockSpec((tm, D), lambda i, s: (i, 0))),
        out_shape=jax.ShapeDtypeStruct((T, D), BF16),
        compiler_params=_cparams("parallel"),
        name="prenorm",
    )(seq_of_tile, x, g, mod)


def _post_kernel(seq_ref, x_ref, y_ref, gpost_ref, mod_ref, *rest, gate, sc, sh, has_next, has_router):
    rest = list(rest)
    gpre_ref = rest.pop(0) if has_next else None
    wr_ref = rest.pop(0) if has_router else None
    xo_ref = rest.pop(0)
    h_ref = rest.pop(0) if has_next else None
    lg_ref = rest.pop(0) if has_router else None
    s = seq_ref[pl.program_id(0)]
    xn = x_ref[...] + _mod_row(mod_ref, gate, s) * _rms(y_ref[...], gpost_ref[...])
    xo_ref[...] = xn
    if has_next:
        h = _rms(xn, gpre_ref[...]) * (1.0 + _mod_row(mod_ref, sc, s)) + _mod_row(mod_ref, sh, s)
        h_ref[...] = h.astype(h_ref.dtype)
        if has_router:
            lg_ref[...] = jnp.dot(h, wr_ref[...], preferred_element_type=F32,
                                  precision=lax.Precision.HIGHEST)


def post_residual(x, y, gpost, mod, seq_of_tile, *, tm, gate, gpre=None, sc=None, sh=None, w_router=None):
    T, D = x.shape
    has_next = gpre is not None
    has_router = w_router is not None
    row = pl.BlockSpec((tm, D), lambda i, s: (i, 0))
    vec = pl.BlockSpec((1, D), lambda i, s: (0, 0))
    in_specs = [row, row, vec, pl.BlockSpec(mod.shape, lambda i, s: (0, 0, 0))]
    args = [x, y, gpost, mod]
    out_shape = [jax.ShapeDtypeStruct((T, D), F32)]
    out_specs = [row]
    if has_next:
        in_specs.append(vec)
        args.append(gpre)
        out_shape.append(jax.ShapeDtypeStruct((T, D), BF16))
        out_specs.append(row)
    if has_router:
        in_specs.append(pl.BlockSpec(w_router.shape, lambda i, s: (0, 0)))
        args.append(w_router)
        out_shape.append(jax.ShapeDtypeStruct((T, w_router.shape[1]), F32))
        out_specs.append(pl.BlockSpec((tm, w_router.shape[1]), lambda i, s: (i, 0)))
    return pl.pallas_call(
        functools.partial(_post_kernel, gate=gate, sc=sc, sh=sh, has_next=has_next, has_router=has_router),
        grid_spec=pltpu.PrefetchScalarGridSpec(
            num_scalar_prefetch=1, grid=(T // tm,), in_specs=in_specs, out_specs=out_specs),
        out_shape=out_shape,
        compiler_params=_cparams("parallel"),
        name="post_residual",
    )(seq_of_tile, *args)


def _attn_kernel(qn_ref, qr_ref, kn_ref, kr_ref, v_ref, *rest, scale, ck):
    o_ref = rest[-1]
    tq = qn_ref.shape[0]
    S = kn_ref.shape[0]
    q = jnp.concatenate([qn_ref[...], qr_ref[...]], axis=1)
    c2 = scale * LOG2_E
    m = jnp.full((tq, 1), -jnp.inf, F32)
    lpart = jnp.zeros((tq, LANES), F32)
    acc = jnp.zeros((tq, V_HEAD_DIM), F32)
    for c in range(S // ck):
        rows = slice(c * ck, (c + 1) * ck)
        k = jnp.concatenate([kn_ref[rows, :], kr_ref[rows, :]], axis=1)
        s = lax.dot_general(q, k, (((1,), (1,)), ((), ())), preferred_element_type=F32)
        mpart = s[:, :LANES]
        for g in range(1, ck // LANES):
            mpart = jnp.maximum(mpart, s[:, g * LANES:(g + 1) * LANES])
        m_new = jnp.maximum(m, jnp.max(mpart, axis=1, keepdims=True))
        alpha = jnp.exp2((m - m_new) * c2)
        p = jnp.exp2(s * c2 - m_new * c2)
        psum = p[:, :LANES]
        for g in range(1, ck // LANES):
            psum = psum + p[:, g * LANES:(g + 1) * LANES]
        lpart = alpha * lpart + psum
        acc = alpha * acc + jnp.dot(p.astype(BF16), v_ref[rows, :], preferred_element_type=F32)
        m = m_new
    l = jnp.sum(lpart, axis=1, keepdims=True)
    o_ref[...] = (acc / l).astype(o_ref.dtype)


def attention(qn, qr, kv, kr, prev, *, n_heads, n_rows, batch, seq, row0, tq, scale, name):
    assert row0 % seq == 0 and seq % tq == 0
    b0 = row0 // seq
    nq = seq // tq
    q0 = row0 // tq
    H = n_heads
    ck = min(512, seq)
    in_specs = [
        pl.BlockSpec((tq, LANES), lambda b, h, i: (q0 + b * nq + i, h)),
        pl.BlockSpec((tq, LANES), lambda b, h, i: (q0 + b * nq + i, h)),
        pl.BlockSpec((seq, LANES), lambda b, h, i: (b0 + b, h)),
        pl.BlockSpec((seq, LANES), lambda b, h, i: (b0 + b, 0)),
        pl.BlockSpec((seq, LANES), lambda b, h, i: (b0 + b, H + h)),
    ]
    args = [qn, qr, kv, kr, kv]
    aliases = {}
    if prev is not None:
        in_specs.append(pl.BlockSpec(memory_space=pl.ANY))
        args.append(prev)
        aliases = {5: 0}
    return pl.pallas_call(
        functools.partial(_attn_kernel, scale=scale, ck=ck),
        grid=(batch, H, nq),
        in_specs=in_specs,
        out_specs=pl.BlockSpec((tq, LANES), lambda b, h, i: (q0 + b * nq + i, h)),
        out_shape=jax.ShapeDtypeStruct((n_rows, H * V_HEAD_DIM), BF16),
        input_output_aliases=aliases,
        compiler_params=_cparams("parallel", "parallel", "arbitrary"),
        name=name,
    )(*args)


def _tile_scan(a_ref, b_ref, n, row0, n_tiles, reverse):
    js = list(range(8))[::-1] if reverse else list(range(8))
    slab = lambda j: (n, pl.ds(row0 + j, n_tiles, stride=8), slice(None))
    A = a_ref[slab(js[0])]
    B = b_ref[slab(js[0])]
    for j in js[1:]:
        aj = a_ref[slab(j)]
        B = aj * B + b_ref[slab(j)]
        A = aj * A
        a_ref[slab(j)] = A
        b_ref[slab(j)] = B


def _rglru_kernel(rx_ref, rg_ref, rp_ref, w_ref, *rest, tt):
    o_ref, xpad, a0, b0, a1, b1 = rest[-6:]
    S, tc = rx_ref.shape
    nb = tc // LANES
    zeros8 = jnp.zeros((8, tc), F32)
    xpad[0:8, :] = zeros8
    xpad[8 + S:16 + S, :] = zeros8
    xpad[8:8 + S, :] = rx_ref[...]
    rp = rp_ref[...]
    cw = [rp[j:j + 1, :] for j in range(4)]
    cb = rp[4:5, :]
    hb_a = (rp[5:6, :], rp[6:7, :])
    hb_i = (rp[7:8, :], rp[8:9, :])
    e2 = tuple((-0.5 * RG_C * LOG2_E) * (jnp.maximum(-lam, 0.0) + jnp.log1p(jnp.exp(-jnp.abs(lam))))
               for lam in (rp[9:10, :], rp[10:11, :]))
    ab = ((a0, b0), (a1, b1))

    def chunk(c, carry):
        c0 = pl.multiple_of(c * tt, tt)
        xw = xpad[pl.ds(c0, tt + 16), :]
        xc = (cw[0] * xw[6:6 + tt] + cw[1] * xw[7:7 + tt] + cw[2] * xw[8:8 + tt] + cw[3] * xw[9:9 + tt]) + cb
        for n in range(nb):
            sl = slice(n * LANES, (n + 1) * LANES)
            xcn = xc[:, sl]
            z = jnp.dot(xcn.astype(BF16), w_ref[n].astype(BF16), preferred_element_type=F32)
            xh = 0.5 * xcn
            for d in range(2):
                th_a = jnp.tanh(z[:, d * LANES:(d + 1) * LANES] + hb_a[d][:, sl])
                th_i = jnp.tanh(z[:, (2 + d) * LANES:(3 + d) * LANES] + hb_i[d][:, sl])
                a = jnp.exp2(e2[d][:, sl] * (1.0 + th_a))
                u = 1.0 - a * a
                gain = u * lax.rsqrt(jnp.maximum(u, 1e-30))
                ab[d][0][n, pl.ds(c0, tt), :] = a
                ab[d][1][n, pl.ds(c0, tt), :] = gain * (1.0 + th_i) * xh
        for n in range(nb):
            for d in range(2):
                _tile_scan(ab[d][0], ab[d][1], n, c0, tt // 8, reverse=(d == 1))
        return carry

    lax.fori_loop(0, S // tt, chunk, 0)

    def scan(t, hs):
        rf = pl.multiple_of(t * 8, 8)
        rb = pl.multiple_of(S - 8 - t * 8, 8)
        out = []
        for n in range(nb):
            hf, hb = hs[2 * n], hs[2 * n + 1]
            h = a0[n, pl.ds(rf, 8), :] * hf + b0[n, pl.ds(rf, 8), :]
            b0[n, pl.ds(rf, 8), :] = h
            out.append(jnp.broadcast_to(h[7:8, :], (8, LANES)))
            h = a1[n, pl.ds(rb, 8), :] * hb + b1[n, pl.ds(rb, 8), :]
            b1[n, pl.ds(rb, 8), :] = h
            out.append(jnp.broadcast_to(h[0:1, :], (8, LANES)))
        return tuple(out)

    h0 = jnp.zeros((8, LANES), F32)
    lax.fori_loop(0, S // 8, scan, (h0,) * (2 * nb), unroll=4)

    def gate(c, carry):
        c0 = pl.multiple_of(c * tt, tt)
        g = rg_ref[pl.ds(c0, tt), :]
        gelu = 0.5 * g * (1.0 + jnp.tanh(0.7978845608028654 * (g + 0.044715 * (g * g * g))))
        hsum = jnp.concatenate([b0[n, pl.ds(c0, tt), :] + b1[n, pl.ds(c0, tt), :] for n in range(nb)], axis=1)
        o_ref[pl.ds(c0, tt), :] = (gelu * hsum).astype(o_ref.dtype)
        return carry

    lax.fori_loop(0, S // tt, gate, 0)


def rglru(r, rp, w_cat, prev, *, n_rows, batch, seq, row0, tc, name):
    C = r.shape[1] // 2
    assert row0 % seq == 0 and C % tc == 0
    b0 = row0 // seq
    nct = C // tc
    tt = min(256, seq)
    in_specs = [
        pl.BlockSpec((seq, tc), lambda b, c: (b0 + b, c)),
        pl.BlockSpec((seq, tc), lambda b, c: (b0 + b, nct + c)),
        pl.BlockSpec((16, tc), lambda b, c: (0, c)),
        pl.BlockSpec((tc // LANES, LANES, 4 * LANES), lambda b, c: (c, 0, 0)),
    ]
    args = [r, r, rp, w_cat]
    aliases = {}
    if prev is not None:
        in_specs.append(pl.BlockSpec(memory_space=pl.ANY))
        args.append(prev)
        aliases = {4: 0}
    return pl.pallas_call(
        functools.partial(_rglru_kernel, tt=tt),
        grid=(batch, nct),
        in_specs=in_specs,
        out_specs=pl.BlockSpec((seq, tc), lambda b, c: (b0 + b, c)),
        out_shape=jax.ShapeDtypeStruct((n_rows, C), BF16),
        scratch_shapes=[pltpu.VMEM((seq + 16, tc), F32)] + [pltpu.VMEM((tc // LANES, seq, LANES), F32)] * 4,
        input_output_aliases=aliases,
        compiler_params=_cparams("parallel", "parallel"),
        name=name,
    )(*args)


def _row_variants(valid, tm, part):
    return [(q, jnp.logical_and(valid > q - part, valid <= q)) for q in range(part, tm + 1, part)]


def _glu_kernel(be_ref, bv_ref, nr_ref, x_ref, wg_ref, wu_ref, o_ref, wgb, wub, *, part):
    del nr_ref
    i = pl.program_id(1)
    changed = jnp.logical_or(i == 0, be_ref[i] != be_ref[jnp.maximum(i - 1, 0)])

    @pl.when(changed)
    def _():
        wgb[...] = wg_ref[...].astype(BF16)
        wub[...] = wu_ref[...].astype(BF16)

    for q, pred in _row_variants(bv_ref[i], x_ref.shape[0], part):
        @pl.when(pred)
        def _(q=q):
            x = x_ref[:q, :]
            g = jnp.dot(x, wgb[...], preferred_element_type=F32)
            u = jnp.dot(x, wub[...], preferred_element_type=F32)
            o_ref[:q, :] = (g * _sigmoid(g) * u).astype(o_ref.dtype)


def glu_up(x, wg, wu, block_e, block_valid, n_real, *, tm, tn):
    P, D = x.shape
    F = wg.shape[2]
    nb = P // tm

    def row(j, i, be, bv, nr):
        return jnp.minimum(i, nr[0] - 1)

    return pl.pallas_call(
        functools.partial(_glu_kernel, part=min(MOE_PART, tm)),
        grid_spec=pltpu.PrefetchScalarGridSpec(
            num_scalar_prefetch=3, grid=(F // tn, nb),
            in_specs=[
                pl.BlockSpec((tm, D), lambda j, i, be, bv, nr: (row(j, i, be, bv, nr), 0)),
                pl.BlockSpec((None, D, tn), lambda j, i, be, bv, nr: (be[row(j, i, be, bv, nr)], 0, j)),
                pl.BlockSpec((None, D, tn), lambda j, i, be, bv, nr: (be[row(j, i, be, bv, nr)], 0, j)),
            ],
            out_specs=pl.BlockSpec((tm, tn), lambda j, i, be, bv, nr: (row(j, i, be, bv, nr), j)),
            scratch_shapes=[pltpu.VMEM((D, tn), BF16), pltpu.VMEM((D, tn), BF16)]),
        out_shape=jax.ShapeDtypeStruct((P, F), BF16),
        compiler_params=_cparams("arbitrary", "arbitrary"),
        name="glu_up",
    )(block_e, block_valid, n_real, x, wg, wu)


def _down_kernel(be_ref, bv_ref, nr_ref, h_ref, w_ref, o_ref, *, part):
    del be_ref, nr_ref
    valid = bv_ref[pl.program_id(1)]

    @pl.when(jnp.logical_and(pl.program_id(2) == 0, valid > 0))
    def _():
        o_ref[...] = jnp.zeros_like(o_ref)

    for q, pred in _row_variants(valid, h_ref.shape[0], part):
        @pl.when(pred)
        def _(q=q):
            o_ref[:q, :] += jnp.dot(h_ref[:q, :], w_ref[...].astype(BF16), preferred_element_type=F32)


def glu_down(h, wd, block_e, block_valid, n_real, *, tm, tn, tk):
    P, F = h.shape
    D = wd.shape[2]
    nb = P // tm

    def row(i, nr):
        return jnp.minimum(i, nr[0] - 1)

    return pl.pallas_call(
        functools.partial(_down_kernel, part=min(MOE_PART, tm)),
        grid_spec=pltpu.PrefetchScalarGridSpec(
            num_scalar_prefetch=3, grid=(D // tn, nb, F // tk),
            in_specs=[
                pl.BlockSpec((tm, tk), lambda n, i, k, be, bv, nr: (row(i, nr), k)),
                pl.BlockSpec((None, tk, tn), lambda n, i, k, be, bv, nr: (be[row(i, nr)], k, n)),
            ],
            out_specs=pl.BlockSpec((tm, tn), lambda n, i, k, be, bv, nr: (row(i, nr), n))),
        out_shape=jax.ShapeDtypeStruct((P, D), F32),
        compiler_params=_cparams("arbitrary", "arbitrary", "arbitrary"),
        name="glu_down",
    )(block_e, block_valid, n_real, h, wd)


def _rot_half_cols(w):
    half = QK_ROPE_DIM // 2
    return jnp.concatenate([-w[..., half:], w[..., :half]], axis=-1)


def _rope_table(S):
    inv = 1.0 / (ROPE_THETA ** (jnp.arange(0, QK_ROPE_DIM, 2, dtype=F32) / QK_ROPE_DIM))
    ang = jnp.arange(S, dtype=F32)[:, None] * inv[None, :]
    c, s = jnp.cos(ang), jnp.sin(ang)
    return jnp.concatenate([c, c, s, s], axis=1)


def _tile_maps(seqs, tm):
    seq_of_tile, tile_pos = [], []
    for sid, (row0, length) in enumerate(seqs):
        assert row0 % tm == 0 and length % tm == 0
        for t in range(length // tm):
            seq_of_tile.append(sid)
            tile_pos.append(t)
    return jnp.asarray(np.array(seq_of_tile, np.int32)), jnp.asarray(np.array(tile_pos, np.int32))


def _moe_plan(logits, n_experts, rows):
    T = logits.shape[0]
    A = T * TOP_K
    top_vals, top_idx = lax.top_k(logits, TOP_K)
    probs = jax.nn.softmax(top_vals, axis=-1)
    flat_e = top_idx.reshape(A).astype(jnp.int32)
    flat_tok = jnp.repeat(jnp.arange(T, dtype=jnp.int32), TOP_K)
    order = jnp.argsort(flat_e)
    sorted_e = flat_e[order]
    counts = jnp.bincount(flat_e, length=n_experts).astype(jnp.int32)
    padded = (counts + rows - 1) // rows * rows
    padded_end = jnp.cumsum(padded)
    start = jnp.cumsum(counts) - counts
    start_pad = padded_end - padded
    dest = start_pad[sorted_e] + jnp.arange(A, dtype=jnp.int32) - start[sorted_e]
    n_blocks = (A + n_experts * (rows - 1) + rows - 1) // rows
    P = n_blocks * rows
    pos = dest[jnp.argsort(order)].reshape(T, TOP_K)
    block_start = jnp.arange(n_blocks, dtype=jnp.int32) * rows
    block_e = jnp.minimum(jnp.searchsorted(padded_end, block_start, side="right"), n_experts - 1).astype(jnp.int32)
    slot = jnp.arange(P, dtype=jnp.int32)
    slot_e = jnp.repeat(block_e, rows)
    k_in = slot - start_pad[slot_e]
    src = jnp.clip(start[slot_e] + k_in, 0, A - 1)
    slot_tok = jnp.where(k_in < counts[slot_e], flat_tok[order][src], 0)
    group_end = start_pad + counts
    block_valid = jnp.clip(group_end[block_e] - block_start, 0, rows)
    block_valid = jnp.where(block_start < padded_end[-1], block_valid, 0).astype(jnp.int32)
    n_real = (padded_end[-1] // rows).astype(jnp.int32)
    return slot_tok, pos, probs, block_e, block_valid, n_real


def _sub_blocks(block_e, block_valid, n_real, rows, sub):
    r = rows // sub
    be = jnp.repeat(block_e, r)
    off = jnp.tile(jnp.arange(r, dtype=jnp.int32) * sub, block_e.shape[0])
    bv = jnp.clip(jnp.repeat(block_valid, r) - off, 0, sub).astype(jnp.int32)
    return be, bv, (n_real * r).reshape(1).astype(jnp.int32)


def kernel(x_prompt, x_sample, c_prompt, c_sample, w_ada, b_ada, norm_mix_pre, norm_mix_post, norm_ffn_pre, norm_ffn_post, w_in, q_norm, w_q_up, kv_norm, w_kv_up, conv_w, conv_b, w_rg_a, b_rg_a, w_rg_i, b_rg_i, rg_lambda, attn_out_norm, rnn_out_norm, w_out, w_ff_gate, w_ff_up, w_ff_down, w_router, w_exp_gate, w_exp_up, w_exp_down):
    B, S, D = x_prompt.shape
    DB, DS, _ = x_sample.shape
    depth = w_in.shape[0]
    T0, T1 = B * S, DB * DS
    T = T0 + T1
    q_rank = q_norm.shape[1]
    kv_rank = kv_norm.shape[1]
    H = w_q_up.shape[2] // (QK_NOPE_DIM + QK_ROPE_DIM)
    C = conv_w.shape[2]
    n_rnn_blocks = w_rg_a.shape[2]
    assert w_rg_a.shape[3] == LANES and C == n_rnn_blocks * LANES
    F_ff = w_ff_gate.shape[2]
    E = w_router.shape[2]
    scale = (QK_NOPE_DIM + QK_ROPE_DIM) ** -0.5

    seqs = [(b * S, S) for b in range(B)] + [(T0 + b * DS, DS) for b in range(DB)]
    s_min = min(S, DS)
    tm = min(1024, s_min)
    te = min(256, s_min)
    _, pos_tm = _tile_maps(seqs, tm)
    seq_te, _ = _tile_maps(seqs, te)
    cs_table = _rope_table(max(S, DS))

    x = jnp.concatenate([x_prompt.reshape(T0, D), x_sample.reshape(T1, D)], axis=0)
    n_seq = B + DB
    c_all = jnp.concatenate([c_prompt, c_sample, jnp.zeros((-n_seq % 16, D), F32)], axis=0)

    def vec(p):
        return p.reshape(1, -1)

    mods = []
    for l in range(depth):
        m = fk_matmul("ada", [c_all], [w_ada], w_lead=l, prologue="silu", bias=vec(b_ada[l]),
                      tm=c_all.shape[0], tn=min(1024, 6 * D), out_dtype=F32)
        mods.append(m.reshape(c_all.shape[0], 6, D).transpose(1, 0, 2))
    mod = jnp.concatenate(mods, axis=0)

    tq = min(512, s_min)
    rows = min(MOE_ROWS, s_min)
    sub = min(MOE_SUB, rows)
    tn_ff = min(512, F_ff)
    tk_ff = min(1024, F_ff)
    tn_dn = min(2048, D)

    hmix = prenorm(x, vec(norm_mix_pre[0]), mod, seq_te, tm=te, sc=1, sh=0)

    for l in range(depth):
        m0 = 6 * l
        wl = w_in[l]
        n_lat = q_rank + kv_rank
        w_kr = wl[:, n_lat:n_lat + QK_ROPE_DIM]
        tn_qkv = next(t for t in (512, 256, 128) if n_lat % t == 0)
        assert q_rank % kv_rank == 0
        w_r = wl[:, n_lat + QK_ROPE_DIM:]
        qkv = fk_matmul("qkv_proj", [hmix], [w_in], w_lead=l, tm=tm, tn=tn_qkv, n_cols=n_lat, out_dtype=F32)
        k_rope = fk_matmul("k_rope", [hmix], [jnp.concatenate([w_kr, _rot_half_cols(w_kr)], axis=1)],
                           tm=tm, tn=LANES, out_dtype=BF16, epilogue="rope", cs=cs_table, tile_pos=pos_tm)
        r = fk_matmul("r_proj", [hmix], [w_r], tm=tm, tn=min(512, 2 * C), out_dtype=F32)

        wq = w_q_up[l].reshape(q_rank, H, QK_NOPE_DIM + QK_ROPE_DIM)
        w_qn = wq[:, :, :QK_NOPE_DIM].reshape(q_rank, H * QK_NOPE_DIM)
        wq_r = wq[:, :, QK_NOPE_DIM:]
        w_qr = jnp.concatenate([wq_r, _rot_half_cols(wq_r)], axis=-1).reshape(q_rank, H * LANES)
        wkv = w_kv_up[l].reshape(kv_rank, H, QK_NOPE_DIM + V_HEAD_DIM)
        w_kv = jnp.concatenate([wkv[:, :, :QK_NOPE_DIM].reshape(kv_rank, H * QK_NOPE_DIM),
                                wkv[:, :, QK_NOPE_DIM:].reshape(kv_rank, H * V_HEAD_DIM)], axis=1)
        qg = vec(q_norm[l])
        q_cols = [(q_rank, 0)]
        kv_cols = [(kv_rank, q_rank // kv_rank)]
        qn = fk_matmul("q_nope", [qkv], [w_qn], x_cols=q_cols, gs=[qg], prologue="rms", tm=tm,
                       tn=min(1024, w_qn.shape[1]), out_dtype=BF16)
        qr = fk_matmul("q_rope", [qkv], [w_qr], x_cols=q_cols, gs=[qg], prologue="rms", tm=tm,
                       tn=min(1024, w_qr.shape[1]), out_dtype=BF16, epilogue="rope",
                       cs=cs_table, tile_pos=pos_tm)
        kv = fk_matmul("kv_up", [qkv], [w_kv], x_cols=kv_cols, gs=[vec(kv_norm[l])], prologue="rms", tm=tm,
                       tn=min(1024, w_kv.shape[1]), out_dtype=BF16)

        y_att = attention(qn, qr, kv, k_rope, None, n_heads=H, n_rows=T, batch=B, seq=S, row0=0,
                          tq=min(tq, S), scale=scale, name="attn_prompt")
        y_att = attention(qn, qr, kv, k_rope, y_att, n_heads=H, n_rows=T, batch=DB, seq=DS, row0=T0,
                          tq=min(tq, DS), scale=scale, name="attn_sample")

        rp = jnp.concatenate([conv_w[l], conv_b[l][None], 0.5 * b_rg_a[l], 0.5 * b_rg_i[l], rg_lambda[l],
                              jnp.zeros((5, C), F32)], axis=0)
        w_cat = 0.5 * jnp.concatenate([w_rg_a[l, 0], w_rg_a[l, 1], w_rg_i[l, 0], w_rg_i[l, 1]], axis=-1)
        tc = min(256, C)
        y_rnn = rglru(r, rp, w_cat, None, n_rows=T, batch=B, seq=S, row0=0, tc=tc, name="rglru_prompt")
        y_rnn = rglru(r, rp, w_cat, y_rnn, n_rows=T, batch=DB, seq=DS, row0=T0, tc=tc, name="rglru_sample")

        assert w_out.shape[1] == 2 * y_att.shape[1] and y_rnn.shape[1] == y_att.shape[1]
        y_mix = fk_matmul("out_proj", [y_att, y_rnn], [w_out, w_out], w_lead=l, w_rows=(0, 1),
                          gs=[vec(attn_out_norm[l]), vec(rnn_out_norm[l])], prologue="rms",
                          tm=tm, tn=min(512, D), out_dtype=F32)

        j = l // 2
        if l % 2 == 0:
            x, hff = post_residual(x, y_mix, vec(norm_mix_post[l]), mod, seq_te, tm=te, gate=m0 + 2,
                                   gpre=vec(norm_ffn_pre[l]), sc=m0 + 4, sh=m0 + 3)
            nb = T // rows
            be = jnp.full((nb,), j, jnp.int32)
            bv = jnp.full((nb,), rows, jnp.int32)
            nr = jnp.full((), nb, jnp.int32)
            sbe, sbv, snr = _sub_blocks(be, bv, nr, rows, sub)
            hidden = glu_up(hff, w_ff_gate, w_ff_up, sbe, sbv, snr, tm=sub, tn=tn_ff)
            f = glu_down(hidden, w_ff_down, be, bv, nr.reshape(1), tm=rows, tn=tn_dn, tk=tk_ff)
        else:
            w_r_pad = jnp.concatenate([w_router[j], jnp.zeros((D, LANES - E), F32)], axis=1)
            x, hff, logits = post_residual(x, y_mix, vec(norm_mix_post[l]), mod, seq_te, tm=te, gate=m0 + 2,
                                           gpre=vec(norm_ffn_pre[l]), sc=m0 + 4, sh=m0 + 3, w_router=w_r_pad)
            slot_tok, pos, probs, be, bv, nr = _moe_plan(logits[:, :E], E, rows)
            be = be + j * E
            sbe, sbv, snr = _sub_blocks(be, bv, nr, rows, sub)
            xs = jnp.take(hff, slot_tok, axis=0)
            n_moe = w_exp_gate.shape[0]
            hidden = glu_up(xs, w_exp_gate.reshape(n_moe * E, D, F_ff), w_exp_up.reshape(n_moe * E, D, F_ff),
                            sbe, sbv, snr, tm=sub, tn=tn_ff)
            ys = glu_down(hidden, w_exp_down.reshape(n_moe * E, F_ff, D), be, bv, nr.reshape(1),
                          tm=rows, tn=tn_dn, tk=tk_ff)
            f = (jnp.take(ys, pos[:, 0], axis=0) * probs[:, 0:1]
                 + jnp.take(ys, pos[:, 1], axis=0) * probs[:, 1:2])

        if l + 1 < depth:
            x, hmix = post_residual(x, f, vec(norm_ffn_post[l]), mod, seq_te, tm=te, gate=m0 + 5,
                                    gpre=vec(norm_mix_pre[l + 1]), sc=m0 + 7, sh=m0 + 6)
        else:
            (x,) = post_residual(x, f, vec(norm_ffn_post[l]), mod, seq_te, tm=te, gate=m0 + 5)

    y_prompt = x[:T0].reshape(B, S, D)
    y_sample = x[T0:].reshape(DB, DS, D)
    return (y_prompt, y_sample)
```

```python
import functools

import numpy as np
import jax
import jax.numpy as jnp
from jax import lax
from jax.experimental import pallas as pl
from jax.experimental.pallas import tpu as pltpu

F32 = jnp.float32
BF16 = jnp.bfloat16

EPS = 1e-6
QK_NOPE_DIM = 128
QK_ROPE_DIM = 64
V_HEAD_DIM = 128
ROPE_THETA = 10000.0
RG_C = 8.0
LOG2_E = 1.4426950408889634
TOP_K = 2
LANES = 128
VMEM_LIMIT = 56 * 1024 * 1024
MOE_ROWS = 1024
MOE_SUB = 512
MOE_PART = 256


def _cparams(*sem):
    return pltpu.CompilerParams(dimension_semantics=sem, vmem_limit_bytes=VMEM_LIMIT)


def _sigmoid(x):
    return 1.0 / (1.0 + jnp.exp(-x))


def _rms(x, g):
    return x * lax.rsqrt(jnp.mean(x * x, axis=-1, keepdims=True) + EPS) * g


def _rope_groups(acc, cs):
    m, n = acc.shape
    reps = n // LANES
    t = acc * (jnp.tile(cs, (1, reps)) if reps > 1 else cs)
    low = lax.broadcasted_iota(jnp.int32, (m, n), 1) % LANES < QK_ROPE_DIM
    outs = []
    for r in range(reps):
        tr = t[:, r * LANES:(r + 1) * LANES]
        outs.append(tr + pltpu.roll(tr, QK_ROPE_DIM, 1))
    y = outs[0] if reps == 1 else jnp.concatenate(outs, axis=1)
    return jnp.where(low, y, 0.0)


def _fk_kernel(pos_ref, *refs, n_in, prologue, has_bias, epilogue):
    del pos_ref
    refs = list(refs)
    xs = [refs.pop(0) for _ in range(n_in)]
    gs = [refs.pop(0) for _ in range(n_in)] if prologue == "rms" else []
    ws = [refs.pop(0) for _ in range(n_in)]
    bias = refs.pop(0) if has_bias else None
    cs = refs.pop(0) if epilogue is not None else None
    out = refs.pop(0)
    xbs = refs

    if prologue is not None:
        @pl.when(pl.program_id(1) == 0)
        def _():
            for i in range(n_in):
                x = xs[i][...].astype(F32)
                if prologue == "rms":
                    x = _rms(x, gs[i][...])
                else:
                    x = x * _sigmoid(x)
                xbs[i][...] = x.astype(BF16)
        lhs = [xb[...] for xb in xbs]
    else:
        lhs = [x[...] for x in xs]

    acc = None
    for i in range(n_in):
        part = jnp.dot(lhs[i], ws[i][...].astype(BF16), preferred_element_type=F32)
        acc = part if acc is None else acc + part
    if has_bias:
        acc = acc + bias[...]
    if epilogue == "rope":
        out[...] = _rope_groups(acc, cs[...]).astype(out.dtype)
    else:
        out[...] = acc.astype(out.dtype)


def fk_matmul(name, xs, ws, *, tm, tn, out_dtype, x_cols=None, gs=None, prologue=None, bias=None,
              w_lead=None, w_rows=None, n_cols=None, epilogue=None, cs=None, tile_pos=None):
    n_in = len(xs)
    M = xs[0].shape[0]
    N = ws[0].shape[-1] if n_cols is None else n_cols
    assert M % tm == 0 and N % tn == 0
    grid = (M // tm, N // tn)
    if tile_pos is None:
        tile_pos = jnp.zeros((grid[0],), jnp.int32)
    if x_cols is None:
        x_cols = [(x.shape[1], 0) for x in xs]

    in_specs, args = [], []
    for x, (kw, cb) in zip(xs, x_cols):
        in_specs.append(pl.BlockSpec((tm, kw), lambda i, j, p, cb=cb: (i, cb)))
        args.append(x)
    if prologue == "rms":
        for g in gs:
            in_specs.append(pl.BlockSpec((1, g.shape[1]), lambda i, j, p: (0, 0)))
            args.append(g)
    for idx, ((kw, _), w) in enumerate(zip(x_cols, ws)):
        if w.ndim == 3:
            rb = 0 if w_rows is None else w_rows[idx]
            in_specs.append(pl.BlockSpec((None, kw, tn), lambda i, j, p, rb=rb: (w_lead, rb, j)))
        else:
            assert w.shape[0] == kw
            in_specs.append(pl.BlockSpec((kw, tn), lambda i, j, p: (0, j)))
        args.append(w)
    if bias is not None:
        in_specs.append(pl.BlockSpec((1, tn), lambda i, j, p: (0, j)))
        args.append(bias)
    if epilogue is not None:
        in_specs.append(pl.BlockSpec((tm, LANES), lambda i, j, p: (p[i], 0)))
        args.append(cs)

    scratch = []
    if prologue is not None:
        scratch = [pltpu.VMEM((tm, kw), BF16) for kw, _ in x_cols]

    kern = functools.partial(_fk_kernel, n_in=n_in, prologue=prologue, has_bias=bias is not None,
                             epilogue=epilogue)
    return pl.pallas_call(
        kern,
        grid_spec=pltpu.PrefetchScalarGridSpec(
            num_scalar_prefetch=1, grid=grid, in_specs=in_specs,
            out_specs=pl.BlockSpec((tm, tn), lambda i, j, p: (i, j)),
            scratch_shapes=scratch),
        out_shape=jax.ShapeDtypeStruct((M, N), out_dtype),
        compiler_params=_cparams("parallel", "arbitrary"),
        name=name,
    )(tile_pos, *args)


def _mod_row(mod_ref, chunk, s):
    return mod_ref[chunk, pl.ds(s, 1), :]


def _split_rows_specs(parts, tm, width):
    specs, start = [], 0
    for p in parts:
        n = p.shape[0] // tm
        specs.append(pl.BlockSpec((tm, width), lambda i, *_, start=start, n=n: (jnp.clip(i - start, 0, n - 1), 0)))
        start += n
    return specs


def _prenorm_kernel(seq_ref, *refs, sc, sh, n_parts, part_tiles):
    xs = refs[:n_parts]
    g_ref, mod_ref, h_ref = refs[n_parts:]
    i = pl.program_id(0)
    s = seq_ref[i]
    x = _select_part(xs, part_tiles, i)
    y = _rms(x, g_ref[...])
    h_ref[...] = (y * (1.0 + _mod_row(mod_ref, sc, s)) + _mod_row(mod_ref, sh, s)).astype(h_ref.dtype)


def _select_part(refs, part_tiles, i):
    x = refs[-1][...]
    end = sum(part_tiles[:-1])
    for k in range(len(refs) - 2, -1, -1):
        x = jnp.where(i < end, refs[k][...], x)
        end -= part_tiles[k]
    return x


def prenorm(xs, g, mod, seq_of_tile, *, tm, sc, sh):
    D = xs[0].shape[1]
    T = sum(x.shape[0] for x in xs)
    part_tiles = tuple(x.shape[0] // tm for x in xs)
    return pl.pallas_call(
        functools.partial(_prenorm_kernel, sc=sc, sh=sh, n_parts=len(xs), part_tiles=part_tiles),
        grid_spec=pltpu.PrefetchScalarGridSpec(
            num_scalar_prefetch=1, grid=(T // tm,),
            in_specs=_split_rows_specs(xs, tm, D) + [
                pl.BlockSpec((1, D), lambda i, s: (0, 0)),
                pl.BlockSpec(mod.shape, lambda i, s: (0, 0, 0))],
            out_specs=pl.BlockSpec((tm, D), lambda i, s: (i, 0))),
        out_shape=jax.ShapeDtypeStruct((T, D), BF16),
        compiler_params=_cparams("parallel"),
        name="prenorm",
    )(seq_of_tile, *xs, g, mod)


def _pack_bf16_pairs(h):
    n = h.shape[1] // 2
    lo = lax.bitcast_convert_type(h[:, :n].astype(BF16).astype(F32), jnp.uint32)
    hi = lax.bitcast_convert_type(h[:, n:].astype(BF16).astype(F32), jnp.uint32)
    return (lo >> 16) | (hi & jnp.uint32(0xFFFF0000))


def _unpack_bf16_pairs(w):
    lo = lax.bitcast_convert_type(w << 16, F32).astype(BF16)
    hi = lax.bitcast_convert_type(w & jnp.uint32(0xFFFF0000), F32).astype(BF16)
    return lo, hi


def _row_dma(src_hbm, row, dst, sem):
    return pltpu.make_async_copy(src_hbm.at[pl.ds(row, 1), :], dst, sem)


def _post_kernel(seq_ref, *refs, gate, sc, sh, n_x, x_tiles, moe, has_next, pack, has_router, out_tiles):
    refs = list(refs)
    xs = [refs.pop(0) for _ in range(n_x)]
    if moe:
        probs_ref, pos_ref, pos_next_ref, ys_hbm = [refs.pop(0) for _ in range(4)]
    else:
        y_ref = refs.pop(0)
    gpost_ref, mod_ref = refs.pop(0), refs.pop(0)
    gpre_ref = refs.pop(0) if has_next else None
    wr_ref = refs.pop(0) if has_router else None
    xo_refs = [refs.pop(0) for _ in out_tiles]
    h_ref = refs.pop(0) if has_next else None
    lg_ref = refs.pop(0) if has_router else None
    i = pl.program_id(0)
    s = seq_ref[i]
    tm = xs[0].shape[0]

    if moe:
        buf, sem = refs
        slot = lax.rem(i, 2)

        def fetch(pos, slot_):
            def body(t, c):
                for k in range(TOP_K):
                    _row_dma(ys_hbm, pos[0, 0, TOP_K * t + k], buf.at[slot_, k, pl.ds(t, 1), :],
                             sem.at[slot_]).start()
                return c
            lax.fori_loop(0, tm, body, 0, unroll=8)

        @pl.when(i == 0)
        def _():
            fetch(pos_ref, 0)

        @pl.when(i + 1 < pl.num_programs(0))
        def _():
            fetch(pos_next_ref, 1 - slot)

        def wait(t, c):
            for k in range(TOP_K):
                _row_dma(ys_hbm, 0, buf.at[slot, k, pl.ds(t, 1), :], sem.at[slot]).wait()
            return c
        lax.fori_loop(0, tm, wait, 0, unroll=8)
        y = probs_ref[:, 0:1] * buf[slot, 0]
        for k in range(1, TOP_K):
            y = y + probs_ref[:, k:k + 1] * buf[slot, k]
    else:
        y = y_ref[...]

    xn = _select_part(xs, x_tiles, i) + _mod_row(mod_ref, gate, s) * _rms(y, gpost_ref[...])
    start = 0
    for xo_ref, n in zip(xo_refs, out_tiles):
        if len(xo_refs) == 1:
            xo_ref[...] = xn
        else:
            @pl.when(jnp.logical_and(i >= start, i < start + n))
            def _(xo_ref=xo_ref):
                xo_ref[...] = xn
        start += n
    if has_next:
        h = _rms(xn, gpre_ref[...]) * (1.0 + _mod_row(mod_ref, sc, s)) + _mod_row(mod_ref, sh, s)
        h_ref[...] = _pack_bf16_pairs(h) if pack else h.astype(h_ref.dtype)
        if has_router:
            lg_ref[...] = jnp.dot(h, wr_ref[...], preferred_element_type=F32,
                                  precision=lax.Precision.HIGHEST)


def post_residual(xs, y, gpost, mod, seq_of_tile, *, tm, gate, gpre=None, sc=None, sh=None, w_router=None,
                  pack=False, moe=None, out_rows=None):
    D = xs[0].shape[1]
    T = sum(x.shape[0] for x in xs)
    nt = T // tm
    has_next = gpre is not None
    has_router = w_router is not None
    row = pl.BlockSpec((tm, D), lambda i, s: (i, 0))
    vec = pl.BlockSpec((1, D), lambda i, s: (0, 0))
    in_specs = _split_rows_specs(xs, tm, D)
    args = list(xs)
    scratch = []
    if moe is not None:
        ys, pos, probs = moe
        pos3 = pos.reshape(nt, 1, TOP_K * tm)
        in_specs += [
            pl.BlockSpec((tm, TOP_K), lambda i, s: (i, 0)),
            pl.BlockSpec((1, 1, TOP_K * tm), lambda i, s: (i, 0, 0), memory_space=pltpu.SMEM),
            pl.BlockSpec((1, 1, TOP_K * tm), lambda i, s: (jnp.minimum(i + 1, nt - 1), 0, 0),
                         memory_space=pltpu.SMEM),
            pl.BlockSpec(memory_space=pl.ANY),
        ]
        args += [probs, pos3, pos3, ys]
        scratch = [pltpu.VMEM((2, TOP_K, tm, D), F32), pltpu.SemaphoreType.DMA((2,))]
    else:
        in_specs.append(row)
        args.append(y)
    in_specs += [vec, pl.BlockSpec(mod.shape, lambda i, s: (0, 0, 0))]
    args += [gpost, mod]
    if has_next:
        in_specs.append(vec)
        args.append(gpre)
    if has_router:
        in_specs.append(pl.BlockSpec(w_router.shape, lambda i, s: (0, 0)))
        args.append(w_router)

    if out_rows is None:
        out_rows = (T,)
    out_tiles = tuple(r // tm for r in out_rows)
    outs = [jax.ShapeDtypeStruct((r, D), F32) for r in out_rows]
    out_shape = list(outs)
    out_specs = _split_rows_specs(outs, tm, D) if len(outs) > 1 else [row]
    if has_next:
        if pack:
            out_shape.append(jax.ShapeDtypeStruct((T, D // 2), jnp.uint32))
            out_specs.append(pl.BlockSpec((tm, D // 2), lambda i, s: (i, 0)))
        else:
            out_shape.append(jax.ShapeDtypeStruct((T, D), BF16))
            out_specs.append(row)
    if has_router:
        out_shape.append(jax.ShapeDtypeStruct((T, w_router.shape[1]), F32))
        out_specs.append(pl.BlockSpec((tm, w_router.shape[1]), lambda i, s: (i, 0)))
    kern = functools.partial(
        _post_kernel, gate=gate, sc=sc, sh=sh, n_x=len(xs), x_tiles=tuple(x.shape[0] // tm for x in xs),
        moe=moe is not None, has_next=has_next, pack=pack, has_router=has_router, out_tiles=out_tiles)
    return pl.pallas_call(
        kern,
        grid_spec=pltpu.PrefetchScalarGridSpec(
            num_scalar_prefetch=1, grid=(nt,), in_specs=in_specs, out_specs=out_specs,
            scratch_shapes=scratch),
        out_shape=out_shape,
        compiler_params=_cparams("arbitrary"),
        name="post_residual",
    )(seq_of_tile, *args)


def _attn_kernel(qn_ref, qr_ref, kn_ref, kr_ref, v_ref, *rest, scale, ck):
    o_ref = rest[-1]
    tq = qn_ref.shape[0]
    S = kn_ref.shape[0]
    q = jnp.concatenate([qn_ref[...], qr_ref[...]], axis=1)
    c2 = scale * LOG2_E
    m = jnp.full((tq, 1), -jnp.inf, F32)
    lpart = jnp.zeros((tq, LANES), F32)
    acc = jnp.zeros((tq, V_HEAD_DIM), F32)
    for c in range(S // ck):
        rows = slice(c * ck, (c + 1) * ck)
        k = jnp.concatenate([kn_ref[rows, :], kr_ref[rows, :]], axis=1)
        s = lax.dot_general(q, k, (((1,), (1,)), ((), ())), preferred_element_type=F32)
        mpart = s[:, :LANES]
        for g in range(1, ck // LANES):
            mpart = jnp.maximum(mpart, s[:, g * LANES:(g + 1) * LANES])
        m_new = jnp.maximum(m, jnp.max(mpart, axis=1, keepdims=True))
        alpha = jnp.exp2((m - m_new) * c2)
        p = jnp.exp2(s * c2 - m_new * c2)
        psum = p[:, :LANES]
        for g in range(1, ck // LANES):
            psum = psum + p[:, g * LANES:(g + 1) * LANES]
        lpart = alpha * lpart + psum
        acc = alpha * acc + jnp.dot(p.astype(BF16), v_ref[rows, :], preferred_element_type=F32)
        m = m_new
    l = jnp.sum(lpart, axis=1, keepdims=True)
    o_ref[...] = (acc / l).astype(o_ref.dtype)


def attention(qn, qr, kv, kr, prev, *, n_heads, n_rows, batch, seq, row0, tq, scale, name):
    assert row0 % seq == 0 and seq % tq == 0
    b0 = row0 // seq
    nq = seq // tq
    q0 = row0 // tq
    H = n_heads
    ck = min(512, seq)
    in_specs = [
        pl.BlockSpec((tq, LANES), lambda b, h, i: (q0 + b * nq + i, h)),
        pl.BlockSpec((tq, LANES), lambda b, h, i: (q0 + b * nq + i, h)),
        pl.BlockSpec((seq, LANES), lambda b, h, i: (b0 + b, h)),
        pl.BlockSpec((seq, LANES), lambda b, h, i: (b0 + b, 0)),
        pl.BlockSpec((seq, LANES), lambda b, h, i: (b0 + b, H + h)),
    ]
    args = [qn, qr, kv, kr, kv]
    aliases = {}
    if prev is not None:
        in_specs.append(pl.BlockSpec(memory_space=pl.ANY))
        args.append(prev)
        aliases = {5: 0}
    return pl.pallas_call(
        functools.partial(_attn_kernel, scale=scale, ck=ck),
        grid=(batch, H, nq),
        in_specs=in_specs,
        out_specs=pl.BlockSpec((tq, LANES), lambda b, h, i: (q0 + b * nq + i, h)),
        out_shape=jax.ShapeDtypeStruct((n_rows, H * V_HEAD_DIM), BF16),
        input_output_aliases=aliases,
        compiler_params=_cparams("parallel", "parallel", "arbitrary"),
        name=name,
    )(*args)


def _tile_scan(a_ref, b_ref, n, row0, n_tiles, reverse):
    js = list(range(8))[::-1] if reverse else list(range(8))
    slab = lambda j: (n, pl.ds(row0 + j, n_tiles, stride=8), slice(None))
    A = a_ref[slab(js[0])]
    B = b_ref[slab(js[0])]
    for j in js[1:]:
        aj = a_ref[slab(j)]
        B = aj * B + b_ref[slab(j)]
        A = aj * A
        a_ref[slab(j)] = A
        b_ref[slab(j)] = B


def _rglru_kernel(rx_ref, rg_ref, rp_ref, w_ref, *rest, tt):
    o_ref, xpad, a0, b0, a1, b1 = rest[-6:]
    S, tc = rx_ref.shape
    nb = tc // LANES
    zeros8 = jnp.zeros((8, tc), F32)
    xpad[0:8, :] = zeros8
    xpad[8 + S:16 + S, :] = zeros8
    xpad[8:8 + S, :] = rx_ref[...]
    rp = rp_ref[...]
    cw = [rp[j:j + 1, :] for j in range(4)]
    cb = rp[4:5, :]
    hb_a = (rp[5:6, :], rp[6:7, :])
    hb_i = (rp[7:8, :], rp[8:9, :])
    e2 = tuple((-0.5 * RG_C * LOG2_E) * (jnp.maximum(-lam, 0.0) + jnp.log1p(jnp.exp(-jnp.abs(lam))))
               for lam in (rp[9:10, :], rp[10:11, :]))
    ab = ((a0, b0), (a1, b1))

    def chunk(c, carry):
        c0 = pl.multiple_of(c * tt, tt)
        xw = xpad[pl.ds(c0, tt + 16), :]
        xc = (cw[0] * xw[6:6 + tt] + cw[1] * xw[7:7 + tt] + cw[2] * xw[8:8 + tt] + cw[3] * xw[9:9 + tt]) + cb
        for n in range(nb):
            sl = slice(n * LANES, (n + 1) * LANES)
            xcn = xc[:, sl]
            z = jnp.dot(xcn.astype(BF16), w_ref[n].astype(BF16), preferred_element_type=F32)
            xh = 0.5 * xcn
            for d in range(2):
                th_a = jnp.tanh(z[:, d * LANES:(d + 1) * LANES] + hb_a[d][:, sl])
                th_i = jnp.tanh(z[:, (2 + d) * LANES:(3 + d) * LANES] + hb_i[d][:, sl])
                a = jnp.exp2(e2[d][:, sl] * (1.0 + th_a))
                u = 1.0 - a * a
                gain = u * lax.rsqrt(jnp.maximum(u, 1e-30))
                ab[d][0][n, pl.ds(c0, tt), :] = a
                ab[d][1][n, pl.ds(c0, tt), :] = gain * (1.0 + th_i) * xh
        for n in range(nb):
            for d in range(2):
                _tile_scan(ab[d][0], ab[d][1], n, c0, tt // 8, reverse=(d == 1))
        return carry

    lax.fori_loop(0, S // tt, chunk, 0)

    def scan(t, hs):
        rf = pl.multiple_of(t * 8, 8)
        rb = pl.multiple_of(S - 8 - t * 8, 8)
        out = []
        for n in range(nb):
            hf, hb = hs[2 * n], hs[2 * n + 1]
            h = a0[n, pl.ds(rf, 8), :] * hf + b0[n, pl.ds(rf, 8), :]
            b0[n, pl.ds(rf, 8), :] = h
            out.append(jnp.broadcast_to(h[7:8, :], (8, LANES)))
            h = a1[n, pl.ds(rb, 8), :] * hb + b1[n, pl.ds(rb, 8), :]
            b1[n, pl.ds(rb, 8), :] = h
            out.append(jnp.broadcast_to(h[0:1, :], (8, LANES)))
        return tuple(out)

    h0 = jnp.zeros((8, LANES), F32)
    lax.fori_loop(0, S // 8, scan, (h0,) * (2 * nb), unroll=4)

    def gate(c, carry):
        c0 = pl.multiple_of(c * tt, tt)
        g = rg_ref[pl.ds(c0, tt), :]
        gelu = 0.5 * g * (1.0 + jnp.tanh(0.7978845608028654 * (g + 0.044715 * (g * g * g))))
        hsum = jnp.concatenate([b0[n, pl.ds(c0, tt), :] + b1[n, pl.ds(c0, tt), :] for n in range(nb)], axis=1)
        o_ref[pl.ds(c0, tt), :] = (gelu * hsum).astype(o_ref.dtype)
        return carry

    lax.fori_loop(0, S // tt, gate, 0)


def rglru(r, rp, w_cat, prev, *, n_rows, batch, seq, row0, tc, name):
    C = r.shape[1] // 2
    assert row0 % seq == 0 and C % tc == 0
    b0 = row0 // seq
    nct = C // tc
    tt = min(256, seq)
    in_specs = [
        pl.BlockSpec((seq, tc), lambda b, c: (b0 + b, c)),
        pl.BlockSpec((seq, tc), lambda b, c: (b0 + b, nct + c)),
        pl.BlockSpec((16, tc), lambda b, c: (0, c)),
        pl.BlockSpec((tc // LANES, LANES, 4 * LANES), lambda b, c: (c, 0, 0)),
    ]
    args = [r, r, rp, w_cat]
    aliases = {}
    if prev is not None:
        in_specs.append(pl.BlockSpec(memory_space=pl.ANY))
        args.append(prev)
        aliases = {4: 0}
    return pl.pallas_call(
        functools.partial(_rglru_kernel, tt=tt),
        grid=(batch, nct),
        in_specs=in_specs,
        out_specs=pl.BlockSpec((seq, tc), lambda b, c: (b0 + b, c)),
        out_shape=jax.ShapeDtypeStruct((n_rows, C), BF16),
        scratch_shapes=[pltpu.VMEM((seq + 16, tc), F32)] + [pltpu.VMEM((tc // LANES, seq, LANES), F32)] * 4,
        input_output_aliases=aliases,
        compiler_params=_cparams("parallel", "parallel"),
        name=name,
    )(*args)


def _row_variants(valid, tm, part):
    return [(q, jnp.logical_and(valid > q - part, valid <= q)) for q in range(part, tm + 1, part)]


def _glu_kernel(be_ref, bv_ref, nr_ref, nxt_ref, x_ref, wg_hbm, wu_hbm, o_ref, land, wb, sem, *, part, tn, packed):
    j, i = pl.program_id(0), pl.program_id(1)
    last = nr_ref[0] - 1
    ie = jnp.minimum(i, last)
    e = be_ref[ie]
    changed = jnp.logical_or(i == 0, e != be_ref[jnp.clip(i - 1, 0, last)])

    def weight_copies(e_, j_):
        col = pl.multiple_of(j_ * tn, tn)
        return [pltpu.make_async_copy(w.at[e_, :, pl.ds(col, tn)], land.at[k], sem.at[k])
                for k, w in enumerate((wg_hbm, wu_hbm))]

    @pl.when(changed)
    def _():
        @pl.when(jnp.logical_and(j == 0, i == 0))
        def _():
            for c in weight_copies(e, j):
                c.start()
        for c in weight_copies(e, j):
            c.wait()
        for k in range(2):
            wb[k] = land[k].astype(BF16)
        nxt = nxt_ref[ie]

        @pl.when(nxt >= 0)
        def _():
            for c in weight_copies(nxt, j):
                c.start()

        @pl.when(jnp.logical_and(nxt < 0, j + 1 < pl.num_programs(0)))
        def _():
            for c in weight_copies(be_ref[0], j + 1):
                c.start()

    for q, pred in _row_variants(bv_ref[i], x_ref.shape[0], part):
        @pl.when(pred)
        def _(q=q):
            if packed:
                half = wb.shape[1] // 2
                x_lo, x_hi = _unpack_bf16_pairs(x_ref[:q, :])
                g, u = [jnp.dot(x_lo, wb[k, :half, :], preferred_element_type=F32)
                        + jnp.dot(x_hi, wb[k, half:, :], preferred_element_type=F32) for k in range(2)]
            else:
                x = x_ref[:q, :]
                g, u = [jnp.dot(x, wb[k], preferred_element_type=F32) for k in range(2)]
            o_ref[:q, :] = (g * _sigmoid(g) * u).astype(o_ref.dtype)


def glu_up(x, wg, wu, block_e, block_valid, n_real, *, tm, tn, packed=False):
    P = x.shape[0]
    _, D, F = wg.shape
    nb = P // tm
    be_c = block_e[jnp.minimum(jnp.arange(nb), n_real[0] - 1)]
    nxt_i = jnp.searchsorted(be_c, be_c, side="right")
    nxt = jnp.where(nxt_i < nb, be_c[jnp.minimum(nxt_i, nb - 1)], -1).astype(jnp.int32)

    def row(i, nr):
        return jnp.minimum(i, nr[0] - 1)

    return pl.pallas_call(
        functools.partial(_glu_kernel, part=min(MOE_PART, tm), tn=tn, packed=packed),
        grid_spec=pltpu.PrefetchScalarGridSpec(
            num_scalar_prefetch=4, grid=(F // tn, nb),
            in_specs=[
                pl.BlockSpec((tm, x.shape[1]), lambda j, i, be, bv, nr, nx: (row(i, nr), 0)),
                pl.BlockSpec(memory_space=pl.ANY),
                pl.BlockSpec(memory_space=pl.ANY),
            ],
            out_specs=pl.BlockSpec((tm, tn), lambda j, i, be, bv, nr, nx: (row(i, nr), j)),
            scratch_shapes=[pltpu.VMEM((2, D, tn), F32), pltpu.VMEM((2, D, tn), BF16),
                            pltpu.SemaphoreType.DMA((2,))]),
        out_shape=jax.ShapeDtypeStruct((P, F), BF16),
        compiler_params=_cparams("arbitrary", "arbitrary"),
        name="glu_up",
    )(block_e, block_valid, n_real, nxt, x, wg, wu)


def _down_kernel(be_ref, bv_ref, nr_ref, h_ref, w_ref, o_ref, *, part):
    del be_ref, nr_ref
    valid = bv_ref[pl.program_id(1)]

    @pl.when(jnp.logical_and(pl.program_id(2) == 0, valid > 0))
    def _():
        o_ref[...] = jnp.zeros_like(o_ref)

    for q, pred in _row_variants(valid, h_ref.shape[0], part):
        @pl.when(pred)
        def _(q=q):
            o_ref[:q, :] += jnp.dot(h_ref[:q, :], w_ref[...].astype(BF16), preferred_element_type=F32)


def glu_down(h, wd, block_e, block_valid, n_real, *, tm, tn, tk):
    P, F = h.shape
    D = wd.shape[2]
    nb = P // tm

    def row(i, nr):
        return jnp.minimum(i, nr[0] - 1)

    return pl.pallas_call(
        functools.partial(_down_kernel, part=min(MOE_PART, tm)),
        grid_spec=pltpu.PrefetchScalarGridSpec(
            num_scalar_prefetch=3, grid=(D // tn, nb, F // tk),
            in_specs=[
                pl.BlockSpec((tm, tk), lambda n, i, k, be, bv, nr: (row(i, nr), k)),
                pl.BlockSpec((None, tk, tn), lambda n, i, k, be, bv, nr: (be[row(i, nr)], k, n)),
            ],
            out_specs=pl.BlockSpec((tm, tn), lambda n, i, k, be, bv, nr: (row(i, nr), n))),
        out_shape=jax.ShapeDtypeStruct((P, D), F32),
        compiler_params=_cparams("arbitrary", "arbitrary", "arbitrary"),
        name="glu_down",
    )(block_e, block_valid, n_real, h, wd)


def _rot_half_cols(w):
    half = QK_ROPE_DIM // 2
    return jnp.concatenate([-w[..., half:], w[..., :half]], axis=-1)


def _rope_table(S):
    inv = 1.0 / (ROPE_THETA ** (jnp.arange(0, QK_ROPE_DIM, 2, dtype=F32) / QK_ROPE_DIM))
    ang = jnp.arange(S, dtype=F32)[:, None] * inv[None, :]
    c, s = jnp.cos(ang), jnp.sin(ang)
    return jnp.concatenate([c, c, s, s], axis=1)


def _tile_maps(seqs, tm):
    seq_of_tile, tile_pos = [], []
    for sid, (row0, length) in enumerate(seqs):
        assert row0 % tm == 0 and length % tm == 0
        for t in range(length // tm):
            seq_of_tile.append(sid)
            tile_pos.append(t)
    return jnp.asarray(np.array(seq_of_tile, np.int32)), jnp.asarray(np.array(tile_pos, np.int32))


def _moe_plan(logits, n_experts, rows):
    T = logits.shape[0]
    A = T * TOP_K
    top_vals, top_idx = lax.top_k(logits, TOP_K)
    probs = jax.nn.softmax(top_vals, axis=-1)
    flat_e = top_idx.reshape(A).astype(jnp.int32)
    flat_tok = jnp.repeat(jnp.arange(T, dtype=jnp.int32), TOP_K)
    order = jnp.argsort(flat_e)
    sorted_e = flat_e[order]
    counts = jnp.bincount(flat_e, length=n_experts).astype(jnp.int32)
    padded = (counts + rows - 1) // rows * rows
    padded_end = jnp.cumsum(padded)
    start = jnp.cumsum(counts) - counts
    start_pad = padded_end - padded
    dest = start_pad[sorted_e] + jnp.arange(A, dtype=jnp.int32) - start[sorted_e]
    n_blocks = (A + n_experts * (rows - 1) + rows - 1) // rows
    P = n_blocks * rows
    pos = dest[jnp.argsort(order)].reshape(T, TOP_K)
    block_start = jnp.arange(n_blocks, dtype=jnp.int32) * rows
    block_e = jnp.minimum(jnp.searchsorted(padded_end, block_start, side="right"), n_experts - 1).astype(jnp.int32)
    slot = jnp.arange(P, dtype=jnp.int32)
    slot_e = jnp.repeat(block_e, rows)
    k_in = slot - start_pad[slot_e]
    src = jnp.clip(start[slot_e] + k_in, 0, A - 1)
    slot_tok = jnp.where(k_in < counts[slot_e], flat_tok[order][src], 0)
    group_end = start_pad + counts
    block_valid = jnp.clip(group_end[block_e] - block_start, 0, rows)
    block_valid = jnp.where(block_start < padded_end[-1], block_valid, 0).astype(jnp.int32)
    n_real = (padded_end[-1] // rows).astype(jnp.int32)
    return slot_tok, pos, probs, block_e, block_valid, n_real


def _sub_blocks(block_e, block_valid, n_real, rows, sub):
    r = rows // sub
    be = jnp.repeat(block_e, r)
    off = jnp.tile(jnp.arange(r, dtype=jnp.int32) * sub, block_e.shape[0])
    bv = jnp.clip(jnp.repeat(block_valid, r) - off, 0, sub).astype(jnp.int32)
    return be, bv, (n_real * r).reshape(1).astype(jnp.int32)


def kernel(x_prompt, x_sample, c_prompt, c_sample, w_ada, b_ada, norm_mix_pre, norm_mix_post, norm_ffn_pre, norm_ffn_post, w_in, q_norm, w_q_up, kv_norm, w_kv_up, conv_w, conv_b, w_rg_a, b_rg_a, w_rg_i, b_rg_i, rg_lambda, attn_out_norm, rnn_out_norm, w_out, w_ff_gate, w_ff_up, w_ff_down, w_router, w_exp_gate, w_exp_up, w_exp_down):
    B, S, D = x_prompt.shape
    DB, DS, _ = x_sample.shape
    depth = w_in.shape[0]
    T0, T1 = B * S, DB * DS
    T = T0 + T1
    q_rank = q_norm.shape[1]
    kv_rank = kv_norm.shape[1]
    H = w_q_up.shape[2] // (QK_NOPE_DIM + QK_ROPE_DIM)
    C = conv_w.shape[2]
    n_rnn_blocks = w_rg_a.shape[2]
    assert w_rg_a.shape[3] == LANES and C == n_rnn_blocks * LANES
    F_ff = w_ff_gate.shape[2]
    E = w_router.shape[2]
    scale = (QK_NOPE_DIM + QK_ROPE_DIM) ** -0.5

    seqs = [(b * S, S) for b in range(B)] + [(T0 + b * DS, DS) for b in range(DB)]
    s_min = min(S, DS)
    tm = min(1024, s_min)
    te = min(256, s_min)
    _, pos_tm = _tile_maps(seqs, tm)
    seq_te, _ = _tile_maps(seqs, te)
    cs_table = _rope_table(max(S, DS))

    xs = [x_prompt.reshape(T0, D), x_sample.reshape(T1, D)]
    n_seq = B + DB
    c_all = jnp.concatenate([c_prompt, c_sample, jnp.zeros((-n_seq % 16, D), F32)], axis=0)

    def vec(p):
        return p.reshape(1, -1)

    mods = []
    for l in range(depth):
        m = fk_matmul("ada", [c_all], [w_ada], w_lead=l, prologue="silu", bias=vec(b_ada[l]),
                      tm=c_all.shape[0], tn=min(1024, 6 * D), out_dtype=F32)
        mods.append(m.reshape(c_all.shape[0], 6, D).transpose(1, 0, 2))
    mod = jnp.concatenate(mods, axis=0)

    tq = min(512, s_min)
    rows = min(MOE_ROWS, s_min)
    sub = min(MOE_SUB, rows)
    tn_ff = min(512, F_ff)
    tk_ff = min(1024, F_ff)
    tn_dn = min(2048, D)

    hmix = prenorm(xs, vec(norm_mix_pre[0]), mod, seq_te, tm=te, sc=1, sh=0)

    for l in range(depth):
        m0 = 6 * l
        wl = w_in[l]
        n_lat = q_rank + kv_rank
        w_kr = wl[:, n_lat:n_lat + QK_ROPE_DIM]
        tn_qkv = next(t for t in (512, 256, 128) if n_lat % t == 0)
        assert q_rank % kv_rank == 0
        w_r = wl[:, n_lat + QK_ROPE_DIM:]
        qkv = fk_matmul("qkv_proj", [hmix], [wl[:, :n_lat]], tm=tm, tn=tn_qkv, out_dtype=F32)
        k_rope = fk_matmul("k_rope", [hmix], [jnp.concatenate([w_kr, _rot_half_cols(w_kr)], axis=1)],
                           tm=tm, tn=LANES, out_dtype=BF16, epilogue="rope", cs=cs_table, tile_pos=pos_tm)
        r = fk_matmul("r_proj", [hmix], [w_r], tm=tm, tn=min(512, 2 * C), out_dtype=F32)

        wq = w_q_up[l].reshape(q_rank, H, QK_NOPE_DIM + QK_ROPE_DIM)
        w_qn = wq[:, :, :QK_NOPE_DIM].reshape(q_rank, H * QK_NOPE_DIM)
        wq_r = wq[:, :, QK_NOPE_DIM:]
        w_qr = jnp.concatenate([wq_r, _rot_half_cols(wq_r)], axis=-1).reshape(q_rank, H * LANES)
        wkv = w_kv_up[l].reshape(kv_rank, H, QK_NOPE_DIM + V_HEAD_DIM)
        w_kv = jnp.concatenate([wkv[:, :, :QK_NOPE_DIM].reshape(kv_rank, H * QK_NOPE_DIM),
                                wkv[:, :, QK_NOPE_DIM:].reshape(kv_rank, H * V_HEAD_DIM)], axis=1)
        qg = vec(q_norm[l])
        q_cols = [(q_rank, 0)]
        kv_cols = [(kv_rank, q_rank // kv_rank)]
        qn = fk_matmul("q_nope", [qkv], [w_qn], x_cols=q_cols, gs=[qg], prologue="rms", tm=tm,
                       tn=min(1024, w_qn.shape[1]), out_dtype=BF16)
        qr = fk_matmul("q_rope", [qkv], [w_qr], x_cols=q_cols, gs=[qg], prologue="rms", tm=tm,
                       tn=min(1024, w_qr.shape[1]), out_dtype=BF16, epilogue="rope",
                       cs=cs_table, tile_pos=pos_tm)
        kv = fk_matmul("kv_up", [qkv], [w_kv], x_cols=kv_cols, gs=[vec(kv_norm[l])], prologue="rms", tm=tm,
                       tn=min(1024, w_kv.shape[1]), out_dtype=BF16)

        y_att = jnp.zeros((T, H * V_HEAD_DIM), BF16)
        y_att = attention(qn, qr, kv, k_rope, y_att, n_heads=H, n_rows=T, batch=B, seq=S, row0=0,
                          tq=min(tq, S), scale=scale, name="attn_prompt")
        y_att = attention(qn, qr, kv, k_rope, y_att, n_heads=H, n_rows=T, batch=DB, seq=DS, row0=T0,
                          tq=min(tq, DS), scale=scale, name="attn_sample")

        rp = jnp.concatenate([conv_w[l], conv_b[l][None], 0.5 * b_rg_a[l], 0.5 * b_rg_i[l], rg_lambda[l],
                              jnp.zeros((5, C), F32)], axis=0)
        w_cat = 0.5 * jnp.concatenate([w_rg_a[l, 0], w_rg_a[l, 1], w_rg_i[l, 0], w_rg_i[l, 1]], axis=-1)
        tc = min(256, C)
        y_rnn = jnp.zeros((T, C), BF16)
        y_rnn = rglru(r, rp, w_cat, y_rnn, n_rows=T, batch=B, seq=S, row0=0, tc=tc, name="rglru_prompt")
        y_rnn = rglru(r, rp, w_cat, y_rnn, n_rows=T, batch=DB, seq=DS, row0=T0, tc=tc, name="rglru_sample")

        assert w_out.shape[1] == 2 * y_att.shape[1] and y_rnn.shape[1] == y_att.shape[1]
        y_mix = fk_matmul("out_proj", [y_att, y_rnn], [w_out, w_out], w_lead=l, w_rows=(0, 1),
                          gs=[vec(attn_out_norm[l]), vec(rnn_out_norm[l])], prologue="rms",
                          tm=tm, tn=min(512, D), out_dtype=F32)

        j = l // 2
        moe = None
        if l % 2 == 0:
            x, hff = post_residual(xs, y_mix, vec(norm_mix_post[l]), mod, seq_te, tm=te, gate=m0 + 2,
                                   gpre=vec(norm_ffn_pre[l]), sc=m0 + 4, sh=m0 + 3)
            nb = T // rows
            be = jnp.full((nb,), j, jnp.int32)
            bv = jnp.full((nb,), rows, jnp.int32)
            nr = jnp.full((), nb, jnp.int32)
            sbe, sbv, snr = _sub_blocks(be, bv, nr, rows, sub)
            hidden = glu_up(hff, w_ff_gate, w_ff_up, sbe, sbv, snr, tm=sub, tn=tn_ff)
            f = glu_down(hidden, w_ff_down, be, bv, nr.reshape(1), tm=rows, tn=tn_dn, tk=tk_ff)
        else:
            w_r_pad = jnp.concatenate([w_router[j], jnp.zeros((D, LANES - E), F32)], axis=1)
            x, hff, logits = post_residual(xs, y_mix, vec(norm_mix_post[l]), mod, seq_te, tm=te, gate=m0 + 2,
                                           gpre=vec(norm_ffn_pre[l]), sc=m0 + 4, sh=m0 + 3, w_router=w_r_pad,
                                           pack=True)
            slot_tok, pos, probs, be, bv, nr = _moe_plan(logits[:, :E], E, rows)
            be = be + j * E
            sbe, sbv, snr = _sub_blocks(be, bv, nr, rows, sub)
            x_sorted = hff.at[slot_tok].get(mode="promise_in_bounds")
            n_moe = w_exp_gate.shape[0]
            hidden = glu_up(x_sorted, w_exp_gate.reshape(n_moe * E, D, F_ff),
                            w_exp_up.reshape(n_moe * E, D, F_ff), sbe, sbv, snr, tm=sub, tn=tn_ff, packed=True)
            ys = glu_down(hidden, w_exp_down.reshape(n_moe * E, F_ff, D), be, bv, nr.reshape(1),
                          tm=rows, tn=tn_dn, tk=tk_ff)
            f, moe = None, (ys, pos, probs)
        xs = [x]

        if l + 1 < depth:
            x, hmix = post_residual(xs, f, vec(norm_ffn_post[l]), mod, seq_te, tm=te, gate=m0 + 5, moe=moe,
                                    gpre=vec(norm_mix_pre[l + 1]), sc=m0 + 7, sh=m0 + 6)
            xs = [x]
        else:
            y_prompt, y_sample = post_residual(xs, f, vec(norm_ffn_post[l]), mod, seq_te, tm=te, gate=m0 + 5,
                                               moe=moe, out_rows=(T0, T1))

    return (y_prompt.reshape(B, S, D), y_sample.reshape(DB, DS, D))
```

```python
import functools

import numpy as np
import jax
import jax.numpy as jnp
from jax import lax
from jax.experimental import pallas as pl
from jax.experimental.pallas import tpu as pltpu

F32 = jnp.float32
BF16 = jnp.bfloat16

EPS = 1e-6
QK_NOPE_DIM = 128
QK_ROPE_DIM = 64
V_HEAD_DIM = 128
ROPE_THETA = 10000.0
RG_C = 8.0
LOG2_E = 1.4426950408889634
TOP_K = 2
LANES = 128
VMEM_LIMIT = 56 * 1024 * 1024
MOE_ROWS = 1024
MOE_SUB = 512
MOE_PART = 256


def _cparams(*sem):
    return pltpu.CompilerParams(dimension_semantics=sem, vmem_limit_bytes=VMEM_LIMIT)


def _sigmoid(x):
    return 1.0 / (1.0 + jnp.exp(-x))


def _rms(x, g):
    return x * lax.rsqrt(jnp.mean(x * x, axis=-1, keepdims=True) + EPS) * g


def _rope_groups(acc, cs):
    m, n = acc.shape
    reps = n // LANES
    t = acc * (jnp.tile(cs, (1, reps)) if reps > 1 else cs)
    low = lax.broadcasted_iota(jnp.int32, (m, n), 1) % LANES < QK_ROPE_DIM
    outs = []
    for r in range(reps):
        tr = t[:, r * LANES:(r + 1) * LANES]
        outs.append(tr + pltpu.roll(tr, QK_ROPE_DIM, 1))
    y = outs[0] if reps == 1 else jnp.concatenate(outs, axis=1)
    return jnp.where(low, y, 0.0)


def _fk_kernel(pos_ref, *refs, n_in, prologue, has_bias, epilogue):
    del pos_ref
    refs = list(refs)
    xs = [refs.pop(0) for _ in range(n_in)]
    gs = [refs.pop(0) for _ in range(n_in)] if prologue == "rms" else []
    ws = [refs.pop(0) for _ in range(n_in)]
    bias = refs.pop(0) if has_bias else None
    cs = refs.pop(0) if epilogue is not None else None
    out = refs.pop(0)
    xbs = refs

    if prologue is not None:
        @pl.when(pl.program_id(1) == 0)
        def _():
            for i in range(n_in):
                x = xs[i][...].astype(F32)
                if prologue == "rms":
                    x = _rms(x, gs[i][...])
                else:
                    x = x * _sigmoid(x)
                xbs[i][...] = x.astype(BF16)
        lhs = [xb[...] for xb in xbs]
    else:
        lhs = [x[...] for x in xs]

    acc = None
    for i in range(n_in):
        part = jnp.dot(lhs[i], ws[i][...].astype(BF16), preferred_element_type=F32)
        acc = part if acc is None else acc + part
    if has_bias:
        acc = acc + bias[...]
    if epilogue == "rope":
        out[...] = _rope_groups(acc, cs[...]).astype(out.dtype)
    else:
        out[...] = acc.astype(out.dtype)


def fk_matmul(name, xs, ws, *, tm, tn, out_dtype, x_cols=None, gs=None, prologue=None, bias=None,
              w_lead=None, w_rows=None, n_cols=None, epilogue=None, cs=None, tile_pos=None):
    n_in = len(xs)
    M = xs[0].shape[0]
    N = ws[0].shape[-1] if n_cols is None else n_cols
    assert M % tm == 0 and N % tn == 0
    grid = (M // tm, N // tn)
    if tile_pos is None:
        tile_pos = jnp.zeros((grid[0],), jnp.int32)
    if x_cols is None:
        x_cols = [(x.shape[1], 0) for x in xs]

    in_specs, args = [], []
    for x, (kw, cb) in zip(xs, x_cols):
        in_specs.append(pl.BlockSpec((tm, kw), lambda i, j, p, cb=cb: (i, cb)))
        args.append(x)
    if prologue == "rms":
        for g in gs:
            in_specs.append(pl.BlockSpec((1, g.shape[1]), lambda i, j, p: (0, 0)))
            args.append(g)
    for idx, ((kw, _), w) in enumerate(zip(x_cols, ws)):
        if w.ndim == 3:
            rb = 0 if w_rows is None else w_rows[idx]
            in_specs.append(pl.BlockSpec((None, kw, tn), lambda i, j, p, rb=rb: (w_lead, rb, j)))
        else:
            assert w.shape[0] == kw
            in_specs.append(pl.BlockSpec((kw, tn), lambda i, j, p: (0, j)))
        args.append(w)
    if bias is not None:
        in_specs.append(pl.BlockSpec((1, tn), lambda i, j, p: (0, j)))
        args.append(bias)
    if epilogue is not None:
        in_specs.append(pl.BlockSpec((tm, LANES), lambda i, j, p: (p[i], 0)))
        args.append(cs)

    scratch = []
    if prologue is not None:
        scratch = [pltpu.VMEM((tm, kw), BF16) for kw, _ in x_cols]

    kern = functools.partial(_fk_kernel, n_in=n_in, prologue=prologue, has_bias=bias is not None,
                             epilogue=epilogue)
    return pl.pallas_call(
        kern,
        grid_spec=pltpu.PrefetchScalarGridSpec(
            num_scalar_prefetch=1, grid=grid, in_specs=in_specs,
            out_specs=pl.BlockSpec((tm, tn), lambda i, j, p: (i, j)),
            scratch_shapes=scratch),
        out_shape=jax.ShapeDtypeStruct((M, N), out_dtype),
        compiler_params=_cparams("parallel", "arbitrary"),
        name=name,
    )(tile_pos, *args)


def _mod_row(mod_ref, chunk, s):
    return mod_ref[chunk, pl.ds(s, 1), :]


def _split_rows_specs(parts, tm, width):
    specs, start = [], 0
    for p in parts:
        n = p.shape[0] // tm
        specs.append(pl.BlockSpec((tm, width), lambda i, *_, start=start, n=n: (jnp.clip(i - start, 0, n - 1), 0)))
        start += n
    return specs


def _prenorm_kernel(seq_ref, *refs, sc, sh, n_parts, part_tiles):
    xs = refs[:n_parts]
    g_ref, mod_ref, h_ref = refs[n_parts:]
    i = pl.program_id(0)
    s = seq_ref[i]
    x = _select_part(xs, part_tiles, i)
    y = _rms(x, g_ref[...])
    h_ref[...] = (y * (1.0 + _mod_row(mod_ref, sc, s)) + _mod_row(mod_ref, sh, s)).astype(h_ref.dtype)


def _select_part(refs, part_tiles, i):
    x = refs[-1][...]
    end = sum(part_tiles[:-1])
    for k in range(len(refs) - 2, -1, -1):
        x = jnp.where(i < end, refs[k][...], x)
        end -= part_tiles[k]
    return x


def prenorm(xs, g, mod, seq_of_tile, *, tm, sc, sh):
    D = xs[0].shape[1]
    T = sum(x.shape[0] for x in xs)
    part_tiles = tuple(x.shape[0] // tm for x in xs)
    return pl.pallas_call(
        functools.partial(_prenorm_kernel, sc=sc, sh=sh, n_parts=len(xs), part_tiles=part_tiles),
        grid_spec=pltpu.PrefetchScalarGridSpec(
            num_scalar_prefetch=1, grid=(T // tm,),
            in_specs=_split_rows_specs(xs, tm, D) + [
                pl.BlockSpec((1, D), lambda i, s: (0, 0)),
                pl.BlockSpec(mod.shape, lambda i, s: (0, 0, 0))],
            out_specs=pl.BlockSpec((tm, D), lambda i, s: (i, 0))),
        out_shape=jax.ShapeDtypeStruct((T, D), BF16),
        compiler_params=_cparams("parallel"),
        name="prenorm",
    )(seq_of_tile, *xs, g, mod)


def _pack_bf16_pairs(h):
    n = h.shape[1] // 2
    lo = lax.bitcast_convert_type(h[:, :n].astype(BF16).astype(F32), jnp.uint32)
    hi = lax.bitcast_convert_type(h[:, n:].astype(BF16).astype(F32), jnp.uint32)
    return (lo >> 16) | (hi & jnp.uint32(0xFFFF0000))


def _unpack_bf16_pairs(w):
    lo = lax.bitcast_convert_type(w << 16, F32).astype(BF16)
    hi = lax.bitcast_convert_type(w & jnp.uint32(0xFFFF0000), F32).astype(BF16)
    return lo, hi


def _row_dma(src_hbm, row, dst, sem):
    return pltpu.make_async_copy(src_hbm.at[pl.ds(row, 1), :], dst, sem)


def _post_kernel(seq_ref, *refs, gate, sc, sh, n_x, x_tiles, moe, has_next, pack, has_router, out_tiles):
    refs = list(refs)
    xs = [refs.pop(0) for _ in range(n_x)]
    if moe:
        probs_ref, pos_ref, pos_next_ref, ys_hbm = [refs.pop(0) for _ in range(4)]
    else:
        y_ref = refs.pop(0)
    gpost_ref, mod_ref = refs.pop(0), refs.pop(0)
    gpre_ref = refs.pop(0) if has_next else None
    wr_ref = refs.pop(0) if has_router else None
    xo_refs = [refs.pop(0) for _ in out_tiles]
    h_ref = refs.pop(0) if has_next else None
    lg_ref = refs.pop(0) if has_router else None
    i = pl.program_id(0)
    s = seq_ref[i]
    tm = xs[0].shape[0]

    if moe:
        buf, sem = refs
        slot = lax.rem(i, 2)

        def fetch(pos, slot_):
            def body(t, c):
                for k in range(TOP_K):
                    _row_dma(ys_hbm, pos[0, 0, TOP_K * t + k], buf.at[slot_, k, pl.ds(t, 1), :],
                             sem.at[slot_]).start()
                return c
            lax.fori_loop(0, tm, body, 0, unroll=8)

        @pl.when(i == 0)
        def _():
            fetch(pos_ref, 0)

        @pl.when(i + 1 < pl.num_programs(0))
        def _():
            fetch(pos_next_ref, 1 - slot)

        def wait(t, c):
            for k in range(TOP_K):
                _row_dma(ys_hbm, 0, buf.at[slot, k, pl.ds(t, 1), :], sem.at[slot]).wait()
            return c
        lax.fori_loop(0, tm, wait, 0, unroll=8)
        y = probs_ref[:, 0:1] * buf[slot, 0]
        for k in range(1, TOP_K):
            y = y + probs_ref[:, k:k + 1] * buf[slot, k]
    else:
        y = y_ref[...]

    xn = _select_part(xs, x_tiles, i) + _mod_row(mod_ref, gate, s) * _rms(y, gpost_ref[...])
    start = 0
    for xo_ref, n in zip(xo_refs, out_tiles):
        if len(xo_refs) == 1:
            xo_ref[...] = xn
        else:
            @pl.when(jnp.logical_and(i >= start, i < start + n))
            def _(xo_ref=xo_ref):
                xo_ref[...] = xn
        start += n
    if has_next:
        h = _rms(xn, gpre_ref[...]) * (1.0 + _mod_row(mod_ref, sc, s)) + _mod_row(mod_ref, sh, s)
        h_ref[...] = _pack_bf16_pairs(h) if pack else h.astype(h_ref.dtype)
        if has_router:
            lg_ref[...] = jnp.dot(h, wr_ref[...], preferred_element_type=F32,
                                  precision=lax.Precision.HIGHEST)


def post_residual(xs, y, gpost, mod, seq_of_tile, *, tm, gate, gpre=None, sc=None, sh=None, w_router=None,
                  pack=False, moe=None, out_rows=None):
    D = xs[0].shape[1]
    T = sum(x.shape[0] for x in xs)
    nt = T // tm
    has_next = gpre is not None
    has_router = w_router is not None
    row = pl.BlockSpec((tm, D), lambda i, s: (i, 0))
    vec = pl.BlockSpec((1, D), lambda i, s: (0, 0))
    in_specs = _split_rows_specs(xs, tm, D)
    args = list(xs)
    scratch = []
    if moe is not None:
        ys, pos, probs = moe
        pos3 = pos.reshape(nt, 1, TOP_K * tm)
        in_specs += [
            pl.BlockSpec((tm, TOP_K), lambda i, s: (i, 0)),
            pl.BlockSpec((1, 1, TOP_K * tm), lambda i, s: (i, 0, 0), memory_space=pltpu.SMEM),
            pl.BlockSpec((1, 1, TOP_K * tm), lambda i, s: (jnp.minimum(i + 1, nt - 1), 0, 0),
                         memory_space=pltpu.SMEM),
            pl.BlockSpec(memory_space=pl.ANY),
        ]
        args += [probs, pos3, pos3, ys]
        scratch = [pltpu.VMEM((2, TOP_K, tm, D), F32), pltpu.SemaphoreType.DMA((2,))]
    else:
        in_specs.append(row)
        args.append(y)
    in_specs += [vec, pl.BlockSpec(mod.shape, lambda i, s: (0, 0, 0))]
    args += [gpost, mod]
    if has_next:
        in_specs.append(vec)
        args.append(gpre)
    if has_router:
        in_specs.append(pl.BlockSpec(w_router.shape, lambda i, s: (0, 0)))
        args.append(w_router)

    if out_rows is None:
        out_rows = (T,)
    out_tiles = tuple(r // tm for r in out_rows)
    outs = [jax.ShapeDtypeStruct((r, D), F32) for r in out_rows]
    out_shape = list(outs)
    out_specs = _split_rows_specs(outs, tm, D) if len(outs) > 1 else [row]
    if has_next:
        if pack:
            out_shape.append(jax.ShapeDtypeStruct((T, D // 2), jnp.uint32))
            out_specs.append(pl.BlockSpec((tm, D // 2), lambda i, s: (i, 0)))
        else:
            out_shape.append(jax.ShapeDtypeStruct((T, D), BF16))
            out_specs.append(row)
    if has_router:
        out_shape.append(jax.ShapeDtypeStruct((T, w_router.shape[1]), F32))
        out_specs.append(pl.BlockSpec((tm, w_router.shape[1]), lambda i, s: (i, 0)))
    kern = functools.partial(
        _post_kernel, gate=gate, sc=sc, sh=sh, n_x=len(xs), x_tiles=tuple(x.shape[0] // tm for x in xs),
        moe=moe is not None, has_next=has_next, pack=pack, has_router=has_router, out_tiles=out_tiles)
    return pl.pallas_call(
        kern,
        grid_spec=pltpu.PrefetchScalarGridSpec(
            num_scalar_prefetch=1, grid=(nt,), in_specs=in_specs, out_specs=out_specs,
            scratch_shapes=scratch),
        out_shape=out_shape,
        compiler_params=_cparams("arbitrary"),
        name="post_residual",
    )(seq_of_tile, *args)


def _attn_kernel(qn_ref, qr_ref, kn_ref, kr_ref, v_ref, *rest, scale, ck):
    o_ref = rest[-1]
    tq = qn_ref.shape[0]
    S = kn_ref.shape[0]
    q = jnp.concatenate([qn_ref[...], qr_ref[...]], axis=1)
    c2 = scale * LOG2_E
    m = jnp.full((tq, 1), -jnp.inf, F32)
    lpart = jnp.zeros((tq, LANES), F32)
    acc = jnp.zeros((tq, V_HEAD_DIM), F32)
    for c in range(S // ck):
        rows = slice(c * ck, (c + 1) * ck)
        k = jnp.concatenate([kn_ref[rows, :], kr_ref[rows, :]], axis=1)
        s = lax.dot_general(q, k, (((1,), (1,)), ((), ())), preferred_element_type=F32)
        mpart = s[:, :LANES]
        for g in range(1, ck // LANES):
            mpart = jnp.maximum(mpart, s[:, g * LANES:(g + 1) * LANES])
        m_new = jnp.maximum(m, jnp.max(mpart, axis=1, keepdims=True))
        alpha = jnp.exp2((m - m_new) * c2)
        p = jnp.exp2(s * c2 - m_new * c2)
        psum = p[:, :LANES]
        for g in range(1, ck // LANES):
            psum = psum + p[:, g * LANES:(g + 1) * LANES]
        lpart = alpha * lpart + psum
        acc = alpha * acc + jnp.dot(p.astype(BF16), v_ref[rows, :], preferred_element_type=F32)
        m = m_new
    l = jnp.sum(lpart, axis=1, keepdims=True)
    o_ref[...] = (acc / l).astype(o_ref.dtype)


def attention(qn, qr, kv, kr, prev, *, n_heads, n_rows, batch, seq, row0, tq, scale, name):
    assert row0 % seq == 0 and seq % tq == 0
    b0 = row0 // seq
    nq = seq // tq
    q0 = row0 // tq
    H = n_heads
    ck = min(512, seq)
    in_specs = [
        pl.BlockSpec((tq, LANES), lambda b, h, i: (q0 + b * nq + i, h)),
        pl.BlockSpec((tq, LANES), lambda b, h, i: (q0 + b * nq + i, h)),
        pl.BlockSpec((seq, LANES), lambda b, h, i: (b0 + b, h)),
        pl.BlockSpec((seq, LANES), lambda b, h, i: (b0 + b, 0)),
        pl.BlockSpec((seq, LANES), lambda b, h, i: (b0 + b, H + h)),
    ]
    args = [qn, qr, kv, kr, kv]
    aliases = {}
    if prev is not None:
        in_specs.append(pl.BlockSpec(memory_space=pl.ANY))
        args.append(prev)
        aliases = {5: 0}
    return pl.pallas_call(
        functools.partial(_attn_kernel, scale=scale, ck=ck),
        grid=(batch, H, nq),
        in_specs=in_specs,
        out_specs=pl.BlockSpec((tq, LANES), lambda b, h, i: (q0 + b * nq + i, h)),
        out_shape=jax.ShapeDtypeStruct((n_rows, H * V_HEAD_DIM), BF16),
        input_output_aliases=aliases,
        compiler_params=_cparams("parallel", "parallel", "arbitrary"),
        name=name,
    )(*args)


def _tile_scan(a_ref, b_ref, n, row0, n_tiles, reverse):
    js = list(range(8))[::-1] if reverse else list(range(8))
    slab = lambda j: (n, pl.ds(row0 + j, n_tiles, stride=8), slice(None))
    A = a_ref[slab(js[0])]
    B = b_ref[slab(js[0])]
    for j in js[1:]:
        aj = a_ref[slab(j)]
        B = aj * B + b_ref[slab(j)]
        A = aj * A
        a_ref[slab(j)] = A
        b_ref[slab(j)] = B


def _rglru_kernel(rx_ref, rg_ref, rp_ref, w_ref, *rest, tt):
    o_ref, xpad, a0, b0, a1, b1 = rest[-6:]
    S, tc = rx_ref.shape
    nb = tc // LANES
    zeros8 = jnp.zeros((8, tc), F32)
    xpad[0:8, :] = zeros8
    xpad[8 + S:16 + S, :] = zeros8
    xpad[8:8 + S, :] = rx_ref[...]
    rp = rp_ref[...]
    cw = [rp[j:j + 1, :] for j in range(4)]
    cb = rp[4:5, :]
    hb_a = (rp[5:6, :], rp[6:7, :])
    hb_i = (rp[7:8, :], rp[8:9, :])
    e2 = tuple((-0.5 * RG_C * LOG2_E) * (jnp.maximum(-lam, 0.0) + jnp.log1p(jnp.exp(-jnp.abs(lam))))
               for lam in (rp[9:10, :], rp[10:11, :]))
    ab = ((a0, b0), (a1, b1))

    def chunk(c, carry):
        c0 = pl.multiple_of(c * tt, tt)
        xw = xpad[pl.ds(c0, tt + 16), :]
        xc = (cw[0] * xw[6:6 + tt] + cw[1] * xw[7:7 + tt] + cw[2] * xw[8:8 + tt] + cw[3] * xw[9:9 + tt]) + cb
        for n in range(nb):
            sl = slice(n * LANES, (n + 1) * LANES)
            xcn = xc[:, sl]
            z = jnp.dot(xcn.astype(BF16), w_ref[n].astype(BF16), preferred_element_type=F32)
            xh = 0.5 * xcn
            for d in range(2):
                th_a = jnp.tanh(z[:, d * LANES:(d + 1) * LANES] + hb_a[d][:, sl])
                th_i = jnp.tanh(z[:, (2 + d) * LANES:(3 + d) * LANES] + hb_i[d][:, sl])
                a = jnp.exp2(e2[d][:, sl] * (1.0 + th_a))
                u = 1.0 - a * a
                gain = u * lax.rsqrt(jnp.maximum(u, 1e-30))
                ab[d][0][n, pl.ds(c0, tt), :] = a
                ab[d][1][n, pl.ds(c0, tt), :] = gain * (1.0 + th_i) * xh
        for n in range(nb):
            for d in range(2):
                _tile_scan(ab[d][0], ab[d][1], n, c0, tt // 8, reverse=(d == 1))
        return carry

    lax.fori_loop(0, S // tt, chunk, 0)

    def scan(t, hs):
        rf = pl.multiple_of(t * 8, 8)
        rb = pl.multiple_of(S - 8 - t * 8, 8)
        out = []
        for n in range(nb):
            hf, hb = hs[2 * n], hs[2 * n + 1]
            h = a0[n, pl.ds(rf, 8), :] * hf + b0[n, pl.ds(rf, 8), :]
            b0[n, pl.ds(rf, 8), :] = h
            out.append(jnp.broadcast_to(h[7:8, :], (8, LANES)))
            h = a1[n, pl.ds(rb, 8), :] * hb + b1[n, pl.ds(rb, 8), :]
            b1[n, pl.ds(rb, 8), :] = h
            out.append(jnp.broadcast_to(h[0:1, :], (8, LANES)))
        return tuple(out)

    h0 = jnp.zeros((8, LANES), F32)
    lax.fori_loop(0, S // 8, scan, (h0,) * (2 * nb), unroll=4)

    def gate(c, carry):
        c0 = pl.multiple_of(c * tt, tt)
        g = rg_ref[pl.ds(c0, tt), :]
        gelu = 0.5 * g * (1.0 + jnp.tanh(0.7978845608028654 * (g + 0.044715 * (g * g * g))))
        hsum = jnp.concatenate([b0[n, pl.ds(c0, tt), :] + b1[n, pl.ds(c0, tt), :] for n in range(nb)], axis=1)
        o_ref[pl.ds(c0, tt), :] = (gelu * hsum).astype(o_ref.dtype)
        return carry

    lax.fori_loop(0, S // tt, gate, 0)


def rglru(r, rp, w_cat, prev, *, n_rows, batch, seq, row0, tc, name):
    C = r.shape[1] // 2
    assert row0 % seq == 0 and C % tc == 0
    b0 = row0 // seq
    nct = C // tc
    tt = min(256, seq)
    in_specs = [
        pl.BlockSpec((seq, tc), lambda b, c: (b0 + b, c)),
        pl.BlockSpec((seq, tc), lambda b, c: (b0 + b, nct + c)),
        pl.BlockSpec((16, tc), lambda b, c: (0, c)),
        pl.BlockSpec((tc // LANES, LANES, 4 * LANES), lambda b, c: (c, 0, 0)),
    ]
    args = [r, r, rp, w_cat]
    aliases = {}
    if prev is not None:
        in_specs.append(pl.BlockSpec(memory_space=pl.ANY))
        args.append(prev)
        aliases = {4: 0}
    return pl.pallas_call(
        functools.partial(_rglru_kernel, tt=tt),
        grid=(batch, nct),
        in_specs=in_specs,
        out_specs=pl.BlockSpec((seq, tc), lambda b, c: (b0 + b, c)),
        out_shape=jax.ShapeDtypeStruct((n_rows, C), BF16),
        scratch_shapes=[pltpu.VMEM((seq + 16, tc), F32)] + [pltpu.VMEM((tc // LANES, seq, LANES), F32)] * 4,
        input_output_aliases=aliases,
        compiler_params=_cparams("parallel", "parallel"),
        name=name,
    )(*args)


def _row_variants(valid, tm, part):
    return [(q, jnp.logical_and(valid > q - part, valid <= q)) for q in range(part, tm + 1, part)]


def _glu_kernel(be_ref, bv_ref, nr_ref, nxt_ref, x_ref, wg_hbm, wu_hbm, o_ref, land, wb, sem, *,
                part, half, tn, packed):
    j, i = pl.program_id(0), pl.program_id(1)
    last = nr_ref[0] - 1
    ie = jnp.minimum(i, last)
    e = be_ref[ie]
    changed = jnp.logical_or(i == 0, e != be_ref[jnp.clip(i - 1, 0, last)])

    def weight_copies(e_, j_):
        col = pl.multiple_of(j_ * tn, tn)
        return [pltpu.make_async_copy(w.at[e_, :, pl.ds(col, tn)], land.at[k], sem.at[k])
                for k, w in enumerate((wg_hbm, wu_hbm))]

    @pl.when(changed)
    def _():
        @pl.when(jnp.logical_and(j == 0, i == 0))
        def _():
            for c in weight_copies(e, j):
                c.start()
        for c in weight_copies(e, j):
            c.wait()
        for k in range(2):
            wb[k] = land[k].astype(BF16)
        nxt = nxt_ref[ie]

        @pl.when(nxt >= 0)
        def _():
            for c in weight_copies(nxt, j):
                c.start()

        @pl.when(jnp.logical_and(nxt < 0, j + 1 < pl.num_programs(0)))
        def _():
            for c in weight_copies(be_ref[0], j + 1):
                c.start()

    valid = bv_ref[i]
    for r0 in range(0, x_ref.shape[0], half):
        v_piece = jnp.clip(valid - r0, 0, half)

        @pl.when(v_piece == 0)
        def _(r0=r0):
            o_ref[r0:r0 + half, :] = jnp.zeros((half, o_ref.shape[1]), o_ref.dtype)

        for q, pred in _row_variants(v_piece, half, part):
            @pl.when(pred)
            def _(q=q, r0=r0):
                if packed:
                    kh = wb.shape[1] // 2
                    x_lo, x_hi = _unpack_bf16_pairs(x_ref[r0:r0 + q, :])
                    g, u = [jnp.dot(x_lo, wb[k, :kh, :], preferred_element_type=F32)
                            + jnp.dot(x_hi, wb[k, kh:, :], preferred_element_type=F32) for k in range(2)]
                else:
                    x = x_ref[r0:r0 + q, :]
                    g, u = [jnp.dot(x, wb[k], preferred_element_type=F32) for k in range(2)]
                o_ref[r0:r0 + q, :] = (g * _sigmoid(g) * u).astype(o_ref.dtype)
                if q < half:
                    o_ref[r0 + q:r0 + half, :] = jnp.zeros((half - q, o_ref.shape[1]), o_ref.dtype)


def glu_up(x, wg, wu, block_e, block_valid, n_real, *, tm, tn, packed=False):
    P = x.shape[0]
    _, D, F = wg.shape
    nb = P // tm
    be_c = block_e[jnp.minimum(jnp.arange(nb), n_real[0] - 1)]
    nxt_i = jnp.searchsorted(be_c, be_c, side="right")
    nxt = jnp.where(nxt_i < nb, be_c[jnp.minimum(nxt_i, nb - 1)], -1).astype(jnp.int32)

    def row(i, nr):
        return jnp.minimum(i, nr[0] - 1)

    return pl.pallas_call(
        functools.partial(_glu_kernel, part=min(MOE_PART, tm), half=min(MOE_SUB, tm), tn=tn, packed=packed),
        grid_spec=pltpu.PrefetchScalarGridSpec(
            num_scalar_prefetch=4, grid=(F // tn, nb),
            in_specs=[
                pl.BlockSpec((tm, x.shape[1]), lambda j, i, be, bv, nr, nx: (row(i, nr), 0)),
                pl.BlockSpec(memory_space=pl.ANY),
                pl.BlockSpec(memory_space=pl.ANY),
            ],
            out_specs=pl.BlockSpec((tm, tn), lambda j, i, be, bv, nr, nx: (i, j)),
            scratch_shapes=[pltpu.VMEM((2, D, tn), F32), pltpu.VMEM((2, D, tn), BF16),
                            pltpu.SemaphoreType.DMA((2,))]),
        out_shape=jax.ShapeDtypeStruct((P, F), BF16),
        compiler_params=_cparams("arbitrary", "arbitrary"),
        name="glu_up",
    )(block_e, block_valid, n_real, nxt, x, wg, wu)


def _down_kernel(be_ref, bv_ref, nr_ref, h_ref, w_ref, o_ref, *, part):
    del be_ref, nr_ref
    valid = bv_ref[pl.program_id(1)]

    @pl.when(pl.program_id(2) == 0)
    def _():
        o_ref[...] = jnp.zeros_like(o_ref)

    for q, pred in _row_variants(valid, h_ref.shape[0], part):
        @pl.when(pred)
        def _(q=q):
            o_ref[:q, :] += jnp.dot(h_ref[:q, :], w_ref[...].astype(BF16), preferred_element_type=F32)


def glu_down(h, wd, block_e, block_valid, n_real, *, tm, tn, tk):
    P, F = h.shape
    D = wd.shape[2]
    nb = P // tm

    def row(i, nr):
        return jnp.minimum(i, nr[0] - 1)

    return pl.pallas_call(
        functools.partial(_down_kernel, part=min(MOE_PART, tm)),
        grid_spec=pltpu.PrefetchScalarGridSpec(
            num_scalar_prefetch=3, grid=(D // tn, nb, F // tk),
            in_specs=[
                pl.BlockSpec((tm, tk), lambda n, i, k, be, bv, nr: (row(i, nr), k)),
                pl.BlockSpec((None, tk, tn), lambda n, i, k, be, bv, nr: (be[row(i, nr)], k, n)),
            ],
            out_specs=pl.BlockSpec((tm, tn), lambda n, i, k, be, bv, nr: (i, n))),
        out_shape=jax.ShapeDtypeStruct((P, D), F32),
        compiler_params=_cparams("arbitrary", "arbitrary", "arbitrary"),
        name="glu_down",
    )(block_e, block_valid, n_real, h, wd)


def _rot_half_cols(w):
    half = QK_ROPE_DIM // 2
    return jnp.concatenate([-w[..., half:], w[..., :half]], axis=-1)


def _rope_table(S):
    inv = 1.0 / (ROPE_THETA ** (jnp.arange(0, QK_ROPE_DIM, 2, dtype=F32) / QK_ROPE_DIM))
    ang = jnp.arange(S, dtype=F32)[:, None] * inv[None, :]
    c, s = jnp.cos(ang), jnp.sin(ang)
    return jnp.concatenate([c, c, s, s], axis=1)


def _tile_maps(seqs, tm):
    seq_of_tile, tile_pos = [], []
    for sid, (row0, length) in enumerate(seqs):
        assert row0 % tm == 0 and length % tm == 0
        for t in range(length // tm):
            seq_of_tile.append(sid)
            tile_pos.append(t)
    return jnp.asarray(np.array(seq_of_tile, np.int32)), jnp.asarray(np.array(tile_pos, np.int32))


def _moe_plan(logits, n_experts, rows):
    T = logits.shape[0]
    A = T * TOP_K
    top_vals, top_idx = lax.top_k(logits, TOP_K)
    probs = jax.nn.softmax(top_vals, axis=-1)
    flat_e = top_idx.reshape(A).astype(jnp.int32)
    flat_tok = jnp.repeat(jnp.arange(T, dtype=jnp.int32), TOP_K)
    order = jnp.argsort(flat_e)
    sorted_e = flat_e[order]
    counts = jnp.bincount(flat_e, length=n_experts).astype(jnp.int32)
    padded = (counts + rows - 1) // rows * rows
    padded_end = jnp.cumsum(padded)
    start = jnp.cumsum(counts) - counts
    start_pad = padded_end - padded
    dest = start_pad[sorted_e] + jnp.arange(A, dtype=jnp.int32) - start[sorted_e]
    n_blocks = (A + n_experts * (rows - 1) + rows - 1) // rows
    P = n_blocks * rows
    pos = dest[jnp.argsort(order)].reshape(T, TOP_K)
    block_start = jnp.arange(n_blocks, dtype=jnp.int32) * rows
    block_e = jnp.minimum(jnp.searchsorted(padded_end, block_start, side="right"), n_experts - 1).astype(jnp.int32)
    slot = jnp.arange(P, dtype=jnp.int32)
    slot_e = jnp.repeat(block_e, rows)
    k_in = slot - start_pad[slot_e]
    src = jnp.clip(start[slot_e] + k_in, 0, A - 1)
    slot_tok = jnp.where(k_in < counts[slot_e], flat_tok[order][src], 0)
    group_end = start_pad + counts
    block_valid = jnp.clip(group_end[block_e] - block_start, 0, rows)
    block_valid = jnp.where(block_start < padded_end[-1], block_valid, 0).astype(jnp.int32)
    n_real = (padded_end[-1] // rows).astype(jnp.int32)
    return slot_tok, pos, probs, block_e, block_valid, n_real


def kernel(x_prompt, x_sample, c_prompt, c_sample, w_ada, b_ada, norm_mix_pre, norm_mix_post, norm_ffn_pre, norm_ffn_post, w_in, q_norm, w_q_up, kv_norm, w_kv_up, conv_w, conv_b, w_rg_a, b_rg_a, w_rg_i, b_rg_i, rg_lambda, attn_out_norm, rnn_out_norm, w_out, w_ff_gate, w_ff_up, w_ff_down, w_router, w_exp_gate, w_exp_up, w_exp_down):
    B, S, D = x_prompt.shape
    DB, DS, _ = x_sample.shape
    depth = w_in.shape[0]
    T0, T1 = B * S, DB * DS
    T = T0 + T1
    q_rank = q_norm.shape[1]
    kv_rank = kv_norm.shape[1]
    H = w_q_up.shape[2] // (QK_NOPE_DIM + QK_ROPE_DIM)
    C = conv_w.shape[2]
    n_rnn_blocks = w_rg_a.shape[2]
    assert w_rg_a.shape[3] == LANES and C == n_rnn_blocks * LANES
    F_ff = w_ff_gate.shape[2]
    E = w_router.shape[2]
    scale = (QK_NOPE_DIM + QK_ROPE_DIM) ** -0.5

    seqs = [(b * S, S) for b in range(B)] + [(T0 + b * DS, DS) for b in range(DB)]
    s_min = min(S, DS)
    tm = min(1024, s_min)
    te = min(256, s_min)
    _, pos_tm = _tile_maps(seqs, tm)
    seq_te, _ = _tile_maps(seqs, te)
    cs_table = _rope_table(max(S, DS))

    xs = [x_prompt.reshape(T0, D), x_sample.reshape(T1, D)]
    n_seq = B + DB
    c_all = jnp.concatenate([c_prompt, c_sample, jnp.zeros((-n_seq % 16, D), F32)], axis=0)

    def vec(p):
        return p.reshape(1, -1)

    mods = []
    for l in range(depth):
        m = fk_matmul("ada", [c_all], [w_ada], w_lead=l, prologue="silu", bias=vec(b_ada[l]),
                      tm=c_all.shape[0], tn=min(1024, 6 * D), out_dtype=F32)
        mods.append(m.reshape(c_all.shape[0], 6, D).transpose(1, 0, 2))
    mod = jnp.concatenate(mods, axis=0)

    tq = min(1024, s_min)
    rows = min(MOE_ROWS, s_min)
    tn_ff = min(512, F_ff)
    tk_ff = min(1024, F_ff)
    tn_dn = min(2048, D)

    hmix = prenorm(xs, vec(norm_mix_pre[0]), mod, seq_te, tm=te, sc=1, sh=0)

    for l in range(depth):
        m0 = 6 * l
        wl = w_in[l]
        n_lat = q_rank + kv_rank
        w_kr = wl[:, n_lat:n_lat + QK_ROPE_DIM]
        tn_qkv = next(t for t in (512, 256, 128) if n_lat % t == 0)
        assert q_rank % kv_rank == 0
        w_r = wl[:, n_lat + QK_ROPE_DIM:]
        qkv = fk_matmul("qkv_proj", [hmix], [wl[:, :n_lat]], tm=tm, tn=tn_qkv, out_dtype=F32)
        k_rope = fk_matmul("k_rope", [hmix], [jnp.concatenate([w_kr, _rot_half_cols(w_kr)], axis=1)],
                           tm=tm, tn=LANES, out_dtype=BF16, epilogue="rope", cs=cs_table, tile_pos=pos_tm)
        r = fk_matmul("r_proj", [hmix], [w_r], tm=tm, tn=min(512, 2 * C), out_dtype=F32)

        wq = w_q_up[l].reshape(q_rank, H, QK_NOPE_DIM + QK_ROPE_DIM)
        w_qn = wq[:, :, :QK_NOPE_DIM].reshape(q_rank, H * QK_NOPE_DIM)
        wq_r = wq[:, :, QK_NOPE_DIM:]
        w_qr = jnp.concatenate([wq_r, _rot_half_cols(wq_r)], axis=-1).reshape(q_rank, H * LANES)
        wkv = w_kv_up[l].reshape(kv_rank, H, QK_NOPE_DIM + V_HEAD_DIM)
        w_kv = jnp.concatenate([wkv[:, :, :QK_NOPE_DIM].reshape(kv_rank, H * QK_NOPE_DIM),
                                wkv[:, :, QK_NOPE_DIM:].reshape(kv_rank, H * V_HEAD_DIM)], axis=1)
        qg = vec(q_norm[l])
        q_cols = [(q_rank, 0)]
        kv_cols = [(kv_rank, q_rank // kv_rank)]
        qn = fk_matmul("q_nope", [qkv], [w_qn], x_cols=q_cols, gs=[qg], prologue="rms", tm=tm,
                       tn=min(1024, w_qn.shape[1]), out_dtype=BF16)
        qr = fk_matmul("q_rope", [qkv], [w_qr], x_cols=q_cols, gs=[qg], prologue="rms", tm=tm,
                       tn=min(1024, w_qr.shape[1]), out_dtype=BF16, epilogue="rope",
                       cs=cs_table, tile_pos=pos_tm)
        kv = fk_matmul("kv_up", [qkv], [w_kv], x_cols=kv_cols, gs=[vec(kv_norm[l])], prologue="rms", tm=tm,
                       tn=min(1024, w_kv.shape[1]), out_dtype=BF16)

        y_att = jnp.zeros((T, H * V_HEAD_DIM), BF16)
        y_att = attention(qn, qr, kv, k_rope, y_att, n_heads=H, n_rows=T, batch=B, seq=S, row0=0,
                          tq=min(tq, S), scale=scale, name="attn_prompt")
        y_att = attention(qn, qr, kv, k_rope, y_att, n_heads=H, n_rows=T, batch=DB, seq=DS, row0=T0,
                          tq=min(tq, DS), scale=scale, name="attn_sample")

        rp = jnp.concatenate([conv_w[l], conv_b[l][None], 0.5 * b_rg_a[l], 0.5 * b_rg_i[l], rg_lambda[l],
                              jnp.zeros((5, C), F32)], axis=0)
        w_cat = 0.5 * jnp.concatenate([w_rg_a[l, 0], w_rg_a[l, 1], w_rg_i[l, 0], w_rg_i[l, 1]], axis=-1)
        tc = min(256, C)
        y_rnn = jnp.zeros((T, C), BF16)
        y_rnn = rglru(r, rp, w_cat, y_rnn, n_rows=T, batch=B, seq=S, row0=0, tc=tc, name="rglru_prompt")
        y_rnn = rglru(r, rp, w_cat, y_rnn, n_rows=T, batch=DB, seq=DS, row0=T0, tc=tc, name="rglru_sample")

        assert w_out.shape[1] == 2 * y_att.shape[1] and y_rnn.shape[1] == y_att.shape[1]
        y_mix = fk_matmul("out_proj", [y_att, y_rnn], [w_out, w_out], w_lead=l, w_rows=(0, 1),
                          gs=[vec(attn_out_norm[l]), vec(rnn_out_norm[l])], prologue="rms",
                          tm=tm, tn=min(512, D), out_dtype=F32)

        j = l // 2
        moe = None
        if l % 2 == 0:
            x, hff = post_residual(xs, y_mix, vec(norm_mix_post[l]), mod, seq_te, tm=te, gate=m0 + 2,
                                   gpre=vec(norm_ffn_pre[l]), sc=m0 + 4, sh=m0 + 3)
            nb = T // rows
            be = jnp.full((nb,), j, jnp.int32)
            bv = jnp.full((nb,), rows, jnp.int32)
            nr = jnp.full((1,), nb, jnp.int32)
            hidden = glu_up(hff, w_ff_gate, w_ff_up, be, bv, nr, tm=rows, tn=tn_ff)
            f = glu_down(hidden, w_ff_down, be, bv, nr, tm=rows, tn=tn_dn, tk=tk_ff)
        else:
            w_r_pad = jnp.concatenate([w_router[j], jnp.zeros((D, LANES - E), F32)], axis=1)
            x, hff, logits = post_residual(xs, y_mix, vec(norm_mix_post[l]), mod, seq_te, tm=te, gate=m0 + 2,
                                           gpre=vec(norm_ffn_pre[l]), sc=m0 + 4, sh=m0 + 3, w_router=w_r_pad,
                                           pack=True)
            slot_tok, pos, probs, be, bv, nr = _moe_plan(logits[:, :E], E, rows)
            be = be + j * E
            nr = nr.reshape(1)
            x_sorted = hff.at[slot_tok].get(mode="promise_in_bounds")
            n_moe = w_exp_gate.shape[0]
            hidden = glu_up(x_sorted, w_exp_gate.reshape(n_moe * E, D, F_ff),
                            w_exp_up.reshape(n_moe * E, D, F_ff), be, bv, nr, tm=rows, tn=tn_ff, packed=True)
            ys = glu_down(hidden, w_exp_down.reshape(n_moe * E, F_ff, D), be, bv, nr,
                          tm=rows, tn=tn_dn, tk=tk_ff)
            f, moe = None, (ys, pos, probs)
        xs = [x]

        if l + 1 < depth:
            x, hmix = post_residual(xs, f, vec(norm_ffn_post[l]), mod, seq_te, tm=te, gate=m0 + 5, moe=moe,
                                    gpre=vec(norm_mix_pre[l + 1]), sc=m0 + 7, sh=m0 + 6)
            xs = [x]
        else:
            y_prompt, y_sample = post_residual(xs, f, vec(norm_ffn_post[l]), mod, seq_te, tm=te, gate=m0 + 5,
                                               moe=moe, out_rows=(T0, T1))

    return (y_prompt.reshape(B, S, D), y_sample.reshape(DB, DS, D))
```

```python
import functools

import numpy as np
import jax
import jax.numpy as jnp
from jax import lax
from jax.experimental import pallas as pl
from jax.experimental.pallas import tpu as pltpu

F32 = jnp.float32
BF16 = jnp.bfloat16

EPS = 1e-6
QK_NOPE_DIM = 128
QK_ROPE_DIM = 64
V_HEAD_DIM = 128
ROPE_THETA = 10000.0
RG_C = 8.0
LOG2_E = 1.4426950408889634
TOP_K = 2
LANES = 128
VMEM_LIMIT = 56 * 1024 * 1024
MOE_ROWS = 1024
MOE_SUB = 512
MOE_PART = 256


def _cparams(*sem):
    return pltpu.CompilerParams(dimension_semantics=sem, vmem_limit_bytes=VMEM_LIMIT)


def _sigmoid(x):
    return 1.0 / (1.0 + jnp.exp(-x))


def _rms(x, g):
    return x * lax.rsqrt(jnp.mean(x * x, axis=-1, keepdims=True) + EPS) * g


def _rope_groups(acc, cs):
    m, n = acc.shape
    reps = n // LANES
    t = acc * (jnp.tile(cs, (1, reps)) if reps > 1 else cs)
    low = lax.broadcasted_iota(jnp.int32, (m, n), 1) % LANES < QK_ROPE_DIM
    outs = []
    for r in range(reps):
        tr = t[:, r * LANES:(r + 1) * LANES]
        outs.append(tr + pltpu.roll(tr, QK_ROPE_DIM, 1))
    y = outs[0] if reps == 1 else jnp.concatenate(outs, axis=1)
    return jnp.where(low, y, 0.0)


def _fk_kernel(pos_ref, *refs, n_in, prologue, has_bias, epilogue):
    del pos_ref
    refs = list(refs)
    xs = [refs.pop(0) for _ in range(n_in)]
    gs = [refs.pop(0) for _ in range(n_in)] if prologue == "rms" else []
    ws = [refs.pop(0) for _ in range(n_in)]
    bias = refs.pop(0) if has_bias else None
    cs = refs.pop(0) if epilogue is not None else None
    out = refs.pop(0)
    xbs = refs

    if prologue is not None:
        @pl.when(pl.program_id(1) == 0)
        def _():
            for i in range(n_in):
                x = xs[i][...].astype(F32)
                if prologue == "rms":
                    x = _rms(x, gs[i][...])
                else:
                    x = x * _sigmoid(x)
                xbs[i][...] = x.astype(BF16)
        lhs = [xb[...] for xb in xbs]
    else:
        lhs = [x[...] for x in xs]

    acc = None
    for i in range(n_in):
        part = jnp.dot(lhs[i], ws[i][...].astype(BF16), preferred_element_type=F32)
        acc = part if acc is None else acc + part
    if has_bias:
        acc = acc + bias[...]
    if epilogue == "rope":
        out[...] = _rope_groups(acc, cs[...]).astype(out.dtype)
    else:
        out[...] = acc.astype(out.dtype)


def fk_matmul(name, xs, ws, *, tm, tn, out_dtype, x_cols=None, gs=None, prologue=None, bias=None,
              w_lead=None, w_rows=None, n_cols=None, epilogue=None, cs=None, tile_pos=None):
    n_in = len(xs)
    M = xs[0].shape[0]
    N = ws[0].shape[-1] if n_cols is None else n_cols
    assert M % tm == 0 and N % tn == 0
    grid = (M // tm, N // tn)
    if tile_pos is None:
        tile_pos = jnp.zeros((grid[0],), jnp.int32)
    if x_cols is None:
        x_cols = [(x.shape[1], 0) for x in xs]

    in_specs, args = [], []
    for x, (kw, cb) in zip(xs, x_cols):
        in_specs.append(pl.BlockSpec((tm, kw), lambda i, j, p, cb=cb: (i, cb)))
        args.append(x)
    if prologue == "rms":
        for g in gs:
            in_specs.append(pl.BlockSpec((1, g.shape[1]), lambda i, j, p: (0, 0)))
            args.append(g)
    for idx, ((kw, _), w) in enumerate(zip(x_cols, ws)):
        if w.ndim == 3:
            rb = 0 if w_rows is None else w_rows[idx]
            in_specs.append(pl.BlockSpec((None, kw, tn), lambda i, j, p, rb=rb: (w_lead, rb, j)))
        else:
            assert w.shape[0] == kw
            in_specs.append(pl.BlockSpec((kw, tn), lambda i, j, p: (0, j)))
        args.append(w)
    if bias is not None:
        in_specs.append(pl.BlockSpec((1, tn), lambda i, j, p: (0, j)))
        args.append(bias)
    if epilogue is not None:
        in_specs.append(pl.BlockSpec((tm, LANES), lambda i, j, p: (p[i], 0)))
        args.append(cs)

    scratch = []
    if prologue is not None:
        scratch = [pltpu.VMEM((tm, kw), BF16) for kw, _ in x_cols]

    kern = functools.partial(_fk_kernel, n_in=n_in, prologue=prologue, has_bias=bias is not None,
                             epilogue=epilogue)
    return pl.pallas_call(
        kern,
        grid_spec=pltpu.PrefetchScalarGridSpec(
            num_scalar_prefetch=1, grid=grid, in_specs=in_specs,
            out_specs=pl.BlockSpec((tm, tn), lambda i, j, p: (i, j)),
            scratch_shapes=scratch),
        out_shape=jax.ShapeDtypeStruct((M, N), out_dtype),
        compiler_params=_cparams("parallel", "arbitrary"),
        name=name,
    )(tile_pos, *args)


def _mod_row(mod_ref, chunk, s):
    return mod_ref[chunk, pl.ds(s, 1), :]


def _split_rows_specs(parts, tm, width):
    specs, start = [], 0
    for p in parts:
        n = p.shape[0] // tm
        specs.append(pl.BlockSpec((tm, width), lambda i, *_, start=start, n=n: (jnp.clip(i - start, 0, n - 1), 0)))
        start += n
    return specs


def _prenorm_kernel(seq_ref, *refs, sc, sh, n_parts, part_tiles):
    xs = refs[:n_parts]
    g_ref, mod_ref, h_ref = refs[n_parts:]
    i = pl.program_id(0)
    s = seq_ref[i]
    x = _select_part(xs, part_tiles, i)
    y = _rms(x, g_ref[...])
    h_ref[...] = (y * (1.0 + _mod_row(mod_ref, sc, s)) + _mod_row(mod_ref, sh, s)).astype(h_ref.dtype)


def _select_part(refs, part_tiles, i):
    x = refs[-1][...]
    end = sum(part_tiles[:-1])
    for k in range(len(refs) - 2, -1, -1):
        x = jnp.where(i < end, refs[k][...], x)
        end -= part_tiles[k]
    return x


def prenorm(xs, g, mod, seq_of_tile, *, tm, sc, sh):
    D = xs[0].shape[1]
    T = sum(x.shape[0] for x in xs)
    part_tiles = tuple(x.shape[0] // tm for x in xs)
    return pl.pallas_call(
        functools.partial(_prenorm_kernel, sc=sc, sh=sh, n_parts=len(xs), part_tiles=part_tiles),
        grid_spec=pltpu.PrefetchScalarGridSpec(
            num_scalar_prefetch=1, grid=(T // tm,),
            in_specs=_split_rows_specs(xs, tm, D) + [
                pl.BlockSpec((1, D), lambda i, s: (0, 0)),
                pl.BlockSpec(mod.shape, lambda i, s: (0, 0, 0))],
            out_specs=pl.BlockSpec((tm, D), lambda i, s: (i, 0))),
        out_shape=jax.ShapeDtypeStruct((T, D), BF16),
        compiler_params=_cparams("parallel"),
        name="prenorm",
    )(seq_of_tile, *xs, g, mod)


def _pack_bf16_pairs(h):
    n = h.shape[1] // 2
    lo = lax.bitcast_convert_type(h[:, :n].astype(BF16).astype(F32), jnp.uint32)
    hi = lax.bitcast_convert_type(h[:, n:].astype(BF16).astype(F32), jnp.uint32)
    return (lo >> 16) | (hi & jnp.uint32(0xFFFF0000))


def _unpack_bf16_pairs(w):
    lo = lax.bitcast_convert_type(w << 16, F32).astype(BF16)
    hi = lax.bitcast_convert_type(w & jnp.uint32(0xFFFF0000), F32).astype(BF16)
    return lo, hi


def _row_dma(src_hbm, row, dst, sem):
    return pltpu.make_async_copy(src_hbm.at[pl.ds(row, 1), :], dst, sem)


def _post_kernel(seq_ref, *refs, gate, sc, sh, n_x, x_tiles, moe, has_next, pack, has_router, out_tiles):
    refs = list(refs)
    xs = [refs.pop(0) for _ in range(n_x)]
    if moe:
        probs_ref, pos_ref, pos_next_ref, ys_hbm = [refs.pop(0) for _ in range(4)]
    else:
        y_ref = refs.pop(0)
    gpost_ref, mod_ref = refs.pop(0), refs.pop(0)
    gpre_ref = refs.pop(0) if has_next else None
    wr_ref = refs.pop(0) if has_router else None
    xo_refs = [refs.pop(0) for _ in out_tiles]
    h_ref = refs.pop(0) if has_next else None
    lg_ref = refs.pop(0) if has_router else None
    i = pl.program_id(0)
    s = seq_ref[i]
    tm = xs[0].shape[0]

    if moe:
        buf, sem = refs
        slot = lax.rem(i, 2)

        def fetch(pos, slot_):
            def body(t, c):
                for k in range(TOP_K):
                    _row_dma(ys_hbm, pos[0, 0, TOP_K * t + k], buf.at[slot_, k, pl.ds(t, 1), :],
                             sem.at[slot_]).start()
                return c
            lax.fori_loop(0, tm, body, 0, unroll=8)

        @pl.when(i == 0)
        def _():
            fetch(pos_ref, 0)

        @pl.when(i + 1 < pl.num_programs(0))
        def _():
            fetch(pos_next_ref, 1 - slot)

        def wait(t, c):
            for k in range(TOP_K):
                _row_dma(ys_hbm, 0, buf.at[slot, k, pl.ds(t, 1), :], sem.at[slot]).wait()
            return c
        lax.fori_loop(0, tm, wait, 0, unroll=8)
        y = probs_ref[:, 0:1] * buf[slot, 0]
        for k in range(1, TOP_K):
            y = y + probs_ref[:, k:k + 1] * buf[slot, k]
    else:
        y = y_ref[...]

    xn = _select_part(xs, x_tiles, i) + _mod_row(mod_ref, gate, s) * _rms(y, gpost_ref[...])
    start = 0
    for xo_ref, n in zip(xo_refs, out_tiles):
        if len(xo_refs) == 1:
            xo_ref[...] = xn
        else:
            @pl.when(jnp.logical_and(i >= start, i < start + n))
            def _(xo_ref=xo_ref):
                xo_ref[...] = xn
        start += n
    if has_next:
        h = _rms(xn, gpre_ref[...]) * (1.0 + _mod_row(mod_ref, sc, s)) + _mod_row(mod_ref, sh, s)
        h_ref[...] = _pack_bf16_pairs(h) if pack else h.astype(h_ref.dtype)
        if has_router:
            lg_ref[...] = jnp.dot(h, wr_ref[...], preferred_element_type=F32,
                                  precision=lax.Precision.HIGHEST)


def post_residual(xs, y, gpost, mod, seq_of_tile, *, tm, gate, gpre=None, sc=None, sh=None, w_router=None,
                  pack=False, moe=None, out_rows=None):
    D = xs[0].shape[1]
    T = sum(x.shape[0] for x in xs)
    nt = T // tm
    has_next = gpre is not None
    has_router = w_router is not None
    row = pl.BlockSpec((tm, D), lambda i, s: (i, 0))
    vec = pl.BlockSpec((1, D), lambda i, s: (0, 0))
    in_specs = _split_rows_specs(xs, tm, D)
    args = list(xs)
    scratch = []
    if moe is not None:
        ys, pos, probs = moe
        pos3 = pos.reshape(nt, 1, TOP_K * tm)
        in_specs += [
            pl.BlockSpec((tm, TOP_K), lambda i, s: (i, 0)),
            pl.BlockSpec((1, 1, TOP_K * tm), lambda i, s: (i, 0, 0), memory_space=pltpu.SMEM),
            pl.BlockSpec((1, 1, TOP_K * tm), lambda i, s: (jnp.minimum(i + 1, nt - 1), 0, 0),
                         memory_space=pltpu.SMEM),
            pl.BlockSpec(memory_space=pl.ANY),
        ]
        args += [probs, pos3, pos3, ys]
        scratch = [pltpu.VMEM((2, TOP_K, tm, D), F32), pltpu.SemaphoreType.DMA((2,))]
    else:
        in_specs.append(row)
        args.append(y)
    in_specs += [vec, pl.BlockSpec(mod.shape, lambda i, s: (0, 0, 0))]
    args += [gpost, mod]
    if has_next:
        in_specs.append(vec)
        args.append(gpre)
    if has_router:
        in_specs.append(pl.BlockSpec(w_router.shape, lambda i, s: (0, 0)))
        args.append(w_router)

    if out_rows is None:
        out_rows = (T,)
    out_tiles = tuple(r // tm for r in out_rows)
    outs = [jax.ShapeDtypeStruct((r, D), F32) for r in out_rows]
    out_shape = list(outs)
    out_specs = _split_rows_specs(outs, tm, D) if len(outs) > 1 else [row]
    if has_next:
        if pack:
            out_shape.append(jax.ShapeDtypeStruct((T, D // 2), jnp.uint32))
            out_specs.append(pl.BlockSpec((tm, D // 2), lambda i, s: (i, 0)))
        else:
            out_shape.append(jax.ShapeDtypeStruct((T, D), BF16))
            out_specs.append(row)
    if has_router:
        out_shape.append(jax.ShapeDtypeStruct((T, w_router.shape[1]), F32))
        out_specs.append(pl.BlockSpec((tm, w_router.shape[1]), lambda i, s: (i, 0)))
    kern = functools.partial(
        _post_kernel, gate=gate, sc=sc, sh=sh, n_x=len(xs), x_tiles=tuple(x.shape[0] // tm for x in xs),
        moe=moe is not None, has_next=has_next, pack=pack, has_router=has_router, out_tiles=out_tiles)
    return pl.pallas_call(
        kern,
        grid_spec=pltpu.PrefetchScalarGridSpec(
            num_scalar_prefetch=1, grid=(nt,), in_specs=in_specs, out_specs=out_specs,
            scratch_shapes=scratch),
        out_shape=out_shape,
        compiler_params=_cparams("arbitrary"),
        name="post_residual",
    )(seq_of_tile, *args)


def _attn_kernel(qn_ref, qr_ref, kn_ref, kr_ref, v_ref, *rest, scale, ck):
    o_ref = rest[-1]
    tq = qn_ref.shape[0]
    S = kn_ref.shape[0]
    q = jnp.concatenate([qn_ref[...], qr_ref[...]], axis=1)
    c2 = scale * LOG2_E
    m = jnp.full((tq, 1), -jnp.inf, F32)
    lpart = jnp.zeros((tq, LANES), F32)
    acc = jnp.zeros((tq, V_HEAD_DIM), F32)
    for c in range(S // ck):
        rows = slice(c * ck, (c + 1) * ck)
        k = jnp.concatenate([kn_ref[rows, :], kr_ref[rows, :]], axis=1)
        s = lax.dot_general(q, k, (((1,), (1,)), ((), ())), preferred_element_type=F32)
        mpart = s[:, :LANES]
        for g in range(1, ck // LANES):
            mpart = jnp.maximum(mpart, s[:, g * LANES:(g + 1) * LANES])
        m_new = jnp.maximum(m, jnp.max(mpart, axis=1, keepdims=True))
        alpha = jnp.exp2((m - m_new) * c2)
        p = jnp.exp2(s * c2 - m_new * c2)
        psum = p[:, :LANES]
        for g in range(1, ck // LANES):
            psum = psum + p[:, g * LANES:(g + 1) * LANES]
        lpart = alpha * lpart + psum
        acc = alpha * acc + jnp.dot(p.astype(BF16), v_ref[rows, :], preferred_element_type=F32)
        m = m_new
    l = jnp.sum(lpart, axis=1, keepdims=True)
    o_ref[...] = (acc / l).astype(o_ref.dtype)


def attention(qn, qr, kv, kr, prev, *, n_heads, n_rows, batch, seq, row0, tq, scale, name):
    assert row0 % seq == 0 and seq % tq == 0
    b0 = row0 // seq
    nq = seq // tq
    q0 = row0 // tq
    H = n_heads
    ck = min(1024, seq)
    in_specs = [
        pl.BlockSpec((tq, LANES), lambda b, h, i: (q0 + b * nq + i, h)),
        pl.BlockSpec((tq, LANES), lambda b, h, i: (q0 + b * nq + i, h)),
        pl.BlockSpec((seq, LANES), lambda b, h, i: (b0 + b, h)),
        pl.BlockSpec((seq, LANES), lambda b, h, i: (b0 + b, 0)),
        pl.BlockSpec((seq, LANES), lambda b, h, i: (b0 + b, H + h)),
    ]
    args = [qn, qr, kv, kr, kv]
    aliases = {}
    if prev is not None:
        in_specs.append(pl.BlockSpec(memory_space=pl.ANY))
        args.append(prev)
        aliases = {5: 0}
    return pl.pallas_call(
        functools.partial(_attn_kernel, scale=scale, ck=ck),
        grid=(batch, H, nq),
        in_specs=in_specs,
        out_specs=pl.BlockSpec((tq, LANES), lambda b, h, i: (q0 + b * nq + i, h)),
        out_shape=jax.ShapeDtypeStruct((n_rows, H * V_HEAD_DIM), BF16),
        input_output_aliases=aliases,
        compiler_params=_cparams("parallel", "parallel", "arbitrary"),
        name=name,
    )(*args)


def _tile_scan(a_ref, b_ref, n, row0, n_tiles, reverse):
    js = list(range(8))[::-1] if reverse else list(range(8))
    slab = lambda j: (n, pl.ds(row0 + j, n_tiles, stride=8), slice(None))
    A = a_ref[slab(js[0])]
    B = b_ref[slab(js[0])]
    for j in js[1:]:
        aj = a_ref[slab(j)]
        B = aj * B + b_ref[slab(j)]
        A = aj * A
        a_ref[slab(j)] = A
        b_ref[slab(j)] = B


def _rglru_kernel(rx_ref, rg_ref, rp_ref, w_ref, *rest, tt):
    o_ref, xpad, a0, b0, a1, b1 = rest[-6:]
    S, tc = rx_ref.shape
    nb = tc // LANES
    zeros8 = jnp.zeros((8, tc), F32)
    xpad[0:8, :] = zeros8
    xpad[8 + S:16 + S, :] = zeros8
    xpad[8:8 + S, :] = rx_ref[...]
    rp = rp_ref[...]
    cw = [rp[j:j + 1, :] for j in range(4)]
    cb = rp[4:5, :]
    hb_a = (rp[5:6, :], rp[6:7, :])
    hb_i = (rp[7:8, :], rp[8:9, :])
    e2 = tuple((-0.5 * RG_C * LOG2_E) * (jnp.maximum(-lam, 0.0) + jnp.log1p(jnp.exp(-jnp.abs(lam))))
               for lam in (rp[9:10, :], rp[10:11, :]))
    ab = ((a0, b0), (a1, b1))

    def chunk(c, carry):
        c0 = pl.multiple_of(c * tt, tt)
        xw = xpad[pl.ds(c0, tt + 16), :]
        xc = (cw[0] * xw[6:6 + tt] + cw[1] * xw[7:7 + tt] + cw[2] * xw[8:8 + tt] + cw[3] * xw[9:9 + tt]) + cb
        for n in range(nb):
            sl = slice(n * LANES, (n + 1) * LANES)
            xcn = xc[:, sl]
            z = jnp.dot(xcn.astype(BF16), w_ref[n].astype(BF16), preferred_element_type=F32)
            xh = 0.5 * xcn
            for d in range(2):
                th_a = jnp.tanh(z[:, d * LANES:(d + 1) * LANES] + hb_a[d][:, sl])
                th_i = jnp.tanh(z[:, (2 + d) * LANES:(3 + d) * LANES] + hb_i[d][:, sl])
                a = jnp.exp2(e2[d][:, sl] * (1.0 + th_a))
                u = 1.0 - a * a
                gain = u * lax.rsqrt(jnp.maximum(u, 1e-30))
                ab[d][0][n, pl.ds(c0, tt), :] = a
                ab[d][1][n, pl.ds(c0, tt), :] = gain * (1.0 + th_i) * xh
        for n in range(nb):
            for d in range(2):
                _tile_scan(ab[d][0], ab[d][1], n, c0, tt // 8, reverse=(d == 1))
        return carry

    lax.fori_loop(0, S // tt, chunk, 0)

    def scan(t, hs):
        rf = pl.multiple_of(t * 8, 8)
        rb = pl.multiple_of(S - 8 - t * 8, 8)
        out = []
        for n in range(nb):
            hf, hb = hs[2 * n], hs[2 * n + 1]
            h = a0[n, pl.ds(rf, 8), :] * hf + b0[n, pl.ds(rf, 8), :]
            b0[n, pl.ds(rf, 8), :] = h
            out.append(jnp.broadcast_to(h[7:8, :], (8, LANES)))
            h = a1[n, pl.ds(rb, 8), :] * hb + b1[n, pl.ds(rb, 8), :]
            b1[n, pl.ds(rb, 8), :] = h
            out.append(jnp.broadcast_to(h[0:1, :], (8, LANES)))
        return tuple(out)

    h0 = jnp.zeros((8, LANES), F32)
    lax.fori_loop(0, S // 8, scan, (h0,) * (2 * nb), unroll=4)

    def gate(c, carry):
        c0 = pl.multiple_of(c * tt, tt)
        g = rg_ref[pl.ds(c0, tt), :]
        gelu = 0.5 * g * (1.0 + jnp.tanh(0.7978845608028654 * (g + 0.044715 * (g * g * g))))
        hsum = jnp.concatenate([b0[n, pl.ds(c0, tt), :] + b1[n, pl.ds(c0, tt), :] for n in range(nb)], axis=1)
        o_ref[pl.ds(c0, tt), :] = (gelu * hsum).astype(o_ref.dtype)
        return carry

    lax.fori_loop(0, S // tt, gate, 0)


def rglru(r, rp, w_cat, prev, *, n_rows, batch, seq, row0, tc, name):
    C = r.shape[1] // 2
    assert row0 % seq == 0 and C % tc == 0
    b0 = row0 // seq
    nct = C // tc
    tt = min(256, seq)
    in_specs = [
        pl.BlockSpec((seq, tc), lambda b, c: (b0 + b, c)),
        pl.BlockSpec((seq, tc), lambda b, c: (b0 + b, nct + c)),
        pl.BlockSpec((16, tc), lambda b, c: (0, c)),
        pl.BlockSpec((tc // LANES, LANES, 4 * LANES), lambda b, c: (c, 0, 0)),
    ]
    args = [r, r, rp, w_cat]
    aliases = {}
    if prev is not None:
        in_specs.append(pl.BlockSpec(memory_space=pl.ANY))
        args.append(prev)
        aliases = {4: 0}
    return pl.pallas_call(
        functools.partial(_rglru_kernel, tt=tt),
        grid=(batch, nct),
        in_specs=in_specs,
        out_specs=pl.BlockSpec((seq, tc), lambda b, c: (b0 + b, c)),
        out_shape=jax.ShapeDtypeStruct((n_rows, C), BF16),
        scratch_shapes=[pltpu.VMEM((seq + 16, tc), F32)] + [pltpu.VMEM((tc // LANES, seq, LANES), F32)] * 4,
        input_output_aliases=aliases,
        compiler_params=_cparams("parallel", "parallel"),
        name=name,
    )(*args)


def _gather_rows_kernel(idx_ref, x_hbm, o_hbm, sem, *, rows):
    base = pl.program_id(0) * rows

    def start(r, c):
        _row_dma(x_hbm, idx_ref[0, 0, r], o_hbm.at[pl.ds(base + r, 1), :], sem).start()
        return c
    lax.fori_loop(0, rows, start, 0, unroll=8)

    def wait(r, c):
        _row_dma(x_hbm, 0, o_hbm.at[pl.ds(base + r, 1), :], sem).wait()
        return c
    lax.fori_loop(0, rows, wait, 0, unroll=8)


def gather_rows(x, idx, *, rows):
    P = idx.shape[0]
    assert P % rows == 0
    return pl.pallas_call(
        functools.partial(_gather_rows_kernel, rows=rows),
        grid=(P // rows,),
        in_specs=[pl.BlockSpec((1, 1, rows), lambda i: (i, 0, 0), memory_space=pltpu.SMEM),
                  pl.BlockSpec(memory_space=pl.ANY)],
        out_specs=pl.BlockSpec(memory_space=pl.ANY),
        out_shape=jax.ShapeDtypeStruct((P, x.shape[1]), x.dtype),
        scratch_shapes=[pltpu.SemaphoreType.DMA(())],
        compiler_params=_cparams("arbitrary"),
        name="gather_rows",
    )(idx.reshape(P // rows, 1, rows), x)


def _row_variants(valid, tm, part):
    return [(q, jnp.logical_and(valid > q - part, valid <= q)) for q in range(part, tm + 1, part)]


def _glu_kernel(be_ref, bv_ref, nr_ref, nxt_ref, x_ref, wg_hbm, wu_hbm, o_ref, land, wb, sem, *,
                part, half, tn, packed):
    j, i = pl.program_id(0), pl.program_id(1)
    last = nr_ref[0] - 1
    ie = jnp.minimum(i, last)
    e = be_ref[ie]
    changed = jnp.logical_or(i == 0, e != be_ref[jnp.clip(i - 1, 0, last)])

    def weight_copies(e_, j_):
        col = pl.multiple_of(j_ * tn, tn)
        return [pltpu.make_async_copy(w.at[e_, :, pl.ds(col, tn)], land.at[k], sem.at[k])
                for k, w in enumerate((wg_hbm, wu_hbm))]

    @pl.when(changed)
    def _():
        @pl.when(jnp.logical_and(j == 0, i == 0))
        def _():
            for c in weight_copies(e, j):
                c.start()
        for c in weight_copies(e, j):
            c.wait()
        for k in range(2):
            wb[k] = land[k].astype(BF16)
        nxt = nxt_ref[ie]

        @pl.when(nxt >= 0)
        def _():
            for c in weight_copies(nxt, j):
                c.start()

        @pl.when(jnp.logical_and(nxt < 0, j + 1 < pl.num_programs(0)))
        def _():
            for c in weight_copies(be_ref[0], j + 1):
                c.start()

    valid = bv_ref[i]
    for r0 in range(0, x_ref.shape[0], half):
        v_piece = jnp.clip(valid - r0, 0, half)

        @pl.when(v_piece == 0)
        def _(r0=r0):
            o_ref[r0:r0 + half, :] = jnp.zeros((half, o_ref.shape[1]), o_ref.dtype)

        for q, pred in _row_variants(v_piece, half, part):
            @pl.when(pred)
            def _(q=q, r0=r0):
                if packed:
                    kh = wb.shape[1] // 2
                    x_lo, x_hi = _unpack_bf16_pairs(x_ref[r0:r0 + q, :])
                    g, u = [jnp.dot(x_lo, wb[k, :kh, :], preferred_element_type=F32)
                            + jnp.dot(x_hi, wb[k, kh:, :], preferred_element_type=F32) for k in range(2)]
                else:
                    x = x_ref[r0:r0 + q, :]
                    g, u = [jnp.dot(x, wb[k], preferred_element_type=F32) for k in range(2)]
                o_ref[r0:r0 + q, :] = (g * _sigmoid(g) * u).astype(o_ref.dtype)
                if q < half:
                    o_ref[r0 + q:r0 + half, :] = jnp.zeros((half - q, o_ref.shape[1]), o_ref.dtype)


def glu_up(x, wg, wu, block_e, block_valid, n_real, *, tm, tn, packed=False):
    P = x.shape[0]
    _, D, F = wg.shape
    nb = P // tm
    be_c = block_e[jnp.minimum(jnp.arange(nb), n_real[0] - 1)]
    nxt_i = jnp.searchsorted(be_c, be_c, side="right")
    nxt = jnp.where(nxt_i < nb, be_c[jnp.minimum(nxt_i, nb - 1)], -1).astype(jnp.int32)

    def row(i, nr):
        return jnp.minimum(i, nr[0] - 1)

    return pl.pallas_call(
        functools.partial(_glu_kernel, part=min(MOE_PART, tm), half=min(MOE_SUB, tm), tn=tn, packed=packed),
        grid_spec=pltpu.PrefetchScalarGridSpec(
            num_scalar_prefetch=4, grid=(F // tn, nb),
            in_specs=[
                pl.BlockSpec((tm, x.shape[1]), lambda j, i, be, bv, nr, nx: (row(i, nr), 0)),
                pl.BlockSpec(memory_space=pl.ANY),
                pl.BlockSpec(memory_space=pl.ANY),
            ],
            out_specs=pl.BlockSpec((tm, tn), lambda j, i, be, bv, nr, nx: (i, j)),
            scratch_shapes=[pltpu.VMEM((2, D, tn), F32), pltpu.VMEM((2, D, tn), BF16),
                            pltpu.SemaphoreType.DMA((2,))]),
        out_shape=jax.ShapeDtypeStruct((P, F), BF16),
        compiler_params=_cparams("arbitrary", "arbitrary"),
        name="glu_up",
    )(block_e, block_valid, n_real, nxt, x, wg, wu)


def _down_kernel(be_ref, bv_ref, nr_ref, h_ref, w_ref, o_ref, *, part):
    del be_ref, nr_ref
    valid = bv_ref[pl.program_id(1)]

    @pl.when(pl.program_id(2) == 0)
    def _():
        o_ref[...] = jnp.zeros_like(o_ref)

    for q, pred in _row_variants(valid, h_ref.shape[0], part):
        @pl.when(pred)
        def _(q=q):
            o_ref[:q, :] += jnp.dot(h_ref[:q, :], w_ref[...].astype(BF16), preferred_element_type=F32)


def glu_down(h, wd, block_e, block_valid, n_real, *, tm, tn, tk):
    P, F = h.shape
    D = wd.shape[2]
    nb = P // tm

    def row(i, nr):
        return jnp.minimum(i, nr[0] - 1)

    return pl.pallas_call(
        functools.partial(_down_kernel, part=min(MOE_PART, tm)),
        grid_spec=pltpu.PrefetchScalarGridSpec(
            num_scalar_prefetch=3, grid=(D // tn, nb, F // tk),
            in_specs=[
                pl.BlockSpec((tm, tk), lambda n, i, k, be, bv, nr: (row(i, nr), k)),
                pl.BlockSpec((None, tk, tn), lambda n, i, k, be, bv, nr: (be[row(i, nr)], k, n)),
            ],
            out_specs=pl.BlockSpec((tm, tn), lambda n, i, k, be, bv, nr: (i, n))),
        out_shape=jax.ShapeDtypeStruct((P, D), F32),
        compiler_params=_cparams("arbitrary", "arbitrary", "arbitrary"),
        name="glu_down",
    )(block_e, block_valid, n_real, h, wd)


def _rot_half_cols(w):
    half = QK_ROPE_DIM // 2
    return jnp.concatenate([-w[..., half:], w[..., :half]], axis=-1)


def _rope_table(S):
    inv = 1.0 / (ROPE_THETA ** (jnp.arange(0, QK_ROPE_DIM, 2, dtype=F32) / QK_ROPE_DIM))
    ang = jnp.arange(S, dtype=F32)[:, None] * inv[None, :]
    c, s = jnp.cos(ang), jnp.sin(ang)
    return jnp.concatenate([c, c, s, s], axis=1)


def _tile_maps(seqs, tm):
    seq_of_tile, tile_pos = [], []
    for sid, (row0, length) in enumerate(seqs):
        assert row0 % tm == 0 and length % tm == 0
        for t in range(length // tm):
            seq_of_tile.append(sid)
            tile_pos.append(t)
    return jnp.asarray(np.array(seq_of_tile, np.int32)), jnp.asarray(np.array(tile_pos, np.int32))


def _moe_plan(logits, n_experts, rows):
    T = logits.shape[0]
    A = T * TOP_K
    top_vals, top_idx = lax.top_k(logits, TOP_K)
    probs = jax.nn.softmax(top_vals, axis=-1)
    flat_e = top_idx.reshape(A).astype(jnp.int32)
    flat_tok = jnp.repeat(jnp.arange(T, dtype=jnp.int32), TOP_K)
    order = jnp.argsort(flat_e)
    sorted_e = flat_e[order]
    counts = jnp.bincount(flat_e, length=n_experts).astype(jnp.int32)
    padded = (counts + rows - 1) // rows * rows
    padded_end = jnp.cumsum(padded)
    start = jnp.cumsum(counts) - counts
    start_pad = padded_end - padded
    dest = start_pad[sorted_e] + jnp.arange(A, dtype=jnp.int32) - start[sorted_e]
    n_blocks = (A + n_experts * (rows - 1) + rows - 1) // rows
    P = n_blocks * rows
    pos = dest[jnp.argsort(order)].reshape(T, TOP_K)
    block_start = jnp.arange(n_blocks, dtype=jnp.int32) * rows
    block_e = jnp.minimum(jnp.searchsorted(padded_end, block_start, side="right"), n_experts - 1).astype(jnp.int32)
    slot = jnp.arange(P, dtype=jnp.int32)
    slot_e = jnp.repeat(block_e, rows)
    k_in = slot - start_pad[slot_e]
    src = jnp.clip(start[slot_e] + k_in, 0, A - 1)
    slot_tok = jnp.where(k_in < counts[slot_e], flat_tok[order][src], 0)
    group_end = start_pad + counts
    block_valid = jnp.clip(group_end[block_e] - block_start, 0, rows)
    block_valid = jnp.where(block_start < padded_end[-1], block_valid, 0).astype(jnp.int32)
    n_real = (padded_end[-1] // rows).astype(jnp.int32)
    return slot_tok, pos, probs, block_e, block_valid, n_real


def kernel(x_prompt, x_sample, c_prompt, c_sample, w_ada, b_ada, norm_mix_pre, norm_mix_post, norm_ffn_pre, norm_ffn_post, w_in, q_norm, w_q_up, kv_norm, w_kv_up, conv_w, conv_b, w_rg_a, b_rg_a, w_rg_i, b_rg_i, rg_lambda, attn_out_norm, rnn_out_norm, w_out, w_ff_gate, w_ff_up, w_ff_down, w_router, w_exp_gate, w_exp_up, w_exp_down):
    B, S, D = x_prompt.shape
    DB, DS, _ = x_sample.shape
    depth = w_in.shape[0]
    T0, T1 = B * S, DB * DS
    T = T0 + T1
    q_rank = q_norm.shape[1]
    kv_rank = kv_norm.shape[1]
    H = w_q_up.shape[2] // (QK_NOPE_DIM + QK_ROPE_DIM)
    C = conv_w.shape[2]
    n_rnn_blocks = w_rg_a.shape[2]
    assert w_rg_a.shape[3] == LANES and C == n_rnn_blocks * LANES
    F_ff = w_ff_gate.shape[2]
    E = w_router.shape[2]
    scale = (QK_NOPE_DIM + QK_ROPE_DIM) ** -0.5

    seqs = [(b * S, S) for b in range(B)] + [(T0 + b * DS, DS) for b in range(DB)]
    s_min = min(S, DS)
    tm = min(1024, s_min)
    te = min(256, s_min)
    _, pos_tm = _tile_maps(seqs, tm)
    seq_te, _ = _tile_maps(seqs, te)
    cs_table = _rope_table(max(S, DS))

    xs = [x_prompt.reshape(T0, D), x_sample.reshape(T1, D)]
    n_seq = B + DB
    c_all = jnp.concatenate([c_prompt, c_sample, jnp.zeros((-n_seq % 16, D), F32)], axis=0)

    def vec(p):
        return p.reshape(1, -1)

    mods = []
    for l in range(depth):
        m = fk_matmul("ada", [c_all], [w_ada], w_lead=l, prologue="silu", bias=vec(b_ada[l]),
                      tm=c_all.shape[0], tn=min(1024, 6 * D), out_dtype=F32)
        mods.append(m.reshape(c_all.shape[0], 6, D).transpose(1, 0, 2))
    mod = jnp.concatenate(mods, axis=0)

    tq = min(1024, s_min)
    rows = min(MOE_ROWS, s_min)
    tn_ff = min(512, F_ff)
    tk_ff = min(1024, F_ff)
    tn_dn = min(2048, D)

    hmix = prenorm(xs, vec(norm_mix_pre[0]), mod, seq_te, tm=te, sc=1, sh=0)

    for l in range(depth):
        m0 = 6 * l
        wl = w_in[l]
        n_lat = q_rank + kv_rank
        w_kr = wl[:, n_lat:n_lat + QK_ROPE_DIM]
        tn_qkv = next(t for t in (512, 256, 128) if n_lat % t == 0)
        assert q_rank % kv_rank == 0
        w_r = wl[:, n_lat + QK_ROPE_DIM:]
        qkv = fk_matmul("qkv_proj", [hmix], [wl[:, :n_lat]], tm=tm, tn=tn_qkv, out_dtype=F32)
        k_rope = fk_matmul("k_rope", [hmix], [jnp.concatenate([w_kr, _rot_half_cols(w_kr)], axis=1)],
                           tm=tm, tn=LANES, out_dtype=BF16, epilogue="rope", cs=cs_table, tile_pos=pos_tm)
        r = fk_matmul("r_proj", [hmix], [w_r], tm=tm, tn=min(512, 2 * C), out_dtype=F32)

        wq = w_q_up[l].reshape(q_rank, H, QK_NOPE_DIM + QK_ROPE_DIM)
        w_qn = wq[:, :, :QK_NOPE_DIM].reshape(q_rank, H * QK_NOPE_DIM)
        wq_r = wq[:, :, QK_NOPE_DIM:]
        w_qr = jnp.concatenate([wq_r, _rot_half_cols(wq_r)], axis=-1).reshape(q_rank, H * LANES)
        wkv = w_kv_up[l].reshape(kv_rank, H, QK_NOPE_DIM + V_HEAD_DIM)
        w_kv = jnp.concatenate([wkv[:, :, :QK_NOPE_DIM].reshape(kv_rank, H * QK_NOPE_DIM),
                                wkv[:, :, QK_NOPE_DIM:].reshape(kv_rank, H * V_HEAD_DIM)], axis=1)
        qg = vec(q_norm[l])
        q_cols = [(q_rank, 0)]
        kv_cols = [(kv_rank, q_rank // kv_rank)]
        qn = fk_matmul("q_nope", [qkv], [w_qn], x_cols=q_cols, gs=[qg], prologue="rms", tm=tm,
                       tn=min(1024, w_qn.shape[1]), out_dtype=BF16)
        qr = fk_matmul("q_rope", [qkv], [w_qr], x_cols=q_cols, gs=[qg], prologue="rms", tm=tm,
                       tn=min(1024, w_qr.shape[1]), out_dtype=BF16, epilogue="rope",
                       cs=cs_table, tile_pos=pos_tm)
        kv = fk_matmul("kv_up", [qkv], [w_kv], x_cols=kv_cols, gs=[vec(kv_norm[l])], prologue="rms", tm=tm,
                       tn=min(1024, w_kv.shape[1]), out_dtype=BF16)

        y_att = jnp.zeros((T, H * V_HEAD_DIM), BF16)
        y_att = attention(qn, qr, kv, k_rope, y_att, n_heads=H, n_rows=T, batch=B, seq=S, row0=0,
                          tq=min(tq, S), scale=scale, name="attn_prompt")
        y_att = attention(qn, qr, kv, k_rope, y_att, n_heads=H, n_rows=T, batch=DB, seq=DS, row0=T0,
                          tq=min(tq, DS), scale=scale, name="attn_sample")

        rp = jnp.concatenate([conv_w[l], conv_b[l][None], 0.5 * b_rg_a[l], 0.5 * b_rg_i[l], rg_lambda[l],
                              jnp.zeros((5, C), F32)], axis=0)
        w_cat = 0.5 * jnp.concatenate([w_rg_a[l, 0], w_rg_a[l, 1], w_rg_i[l, 0], w_rg_i[l, 1]], axis=-1)
        tc = min(256, C)
        y_rnn = jnp.zeros((T, C), BF16)
        y_rnn = rglru(r, rp, w_cat, y_rnn, n_rows=T, batch=B, seq=S, row0=0, tc=tc, name="rglru_prompt")
        y_rnn = rglru(r, rp, w_cat, y_rnn, n_rows=T, batch=DB, seq=DS, row0=T0, tc=tc, name="rglru_sample")

        assert w_out.shape[1] == 2 * y_att.shape[1] and y_rnn.shape[1] == y_att.shape[1]
        y_mix = fk_matmul("out_proj", [y_att, y_rnn], [w_out, w_out], w_lead=l, w_rows=(0, 1),
                          gs=[vec(attn_out_norm[l]), vec(rnn_out_norm[l])], prologue="rms",
                          tm=tm, tn=min(512, D), out_dtype=F32)

        j = l // 2
        moe = None
        if l % 2 == 0:
            x, hff = post_residual(xs, y_mix, vec(norm_mix_post[l]), mod, seq_te, tm=te, gate=m0 + 2,
                                   gpre=vec(norm_ffn_pre[l]), sc=m0 + 4, sh=m0 + 3)
            nb = T // rows
            be = jnp.full((nb,), j, jnp.int32)
            bv = jnp.full((nb,), rows, jnp.int32)
            nr = jnp.full((1,), nb, jnp.int32)
            hidden = glu_up(hff, w_ff_gate, w_ff_up, be, bv, nr, tm=rows, tn=tn_ff)
            f = glu_down(hidden, w_ff_down, be, bv, nr, tm=rows, tn=tn_dn, tk=tk_ff)
        else:
            w_r_pad = jnp.concatenate([w_router[j], jnp.zeros((D, LANES - E), F32)], axis=1)
            x, hff, logits = post_residual(xs, y_mix, vec(norm_mix_post[l]), mod, seq_te, tm=te, gate=m0 + 2,
                                           gpre=vec(norm_ffn_pre[l]), sc=m0 + 4, sh=m0 + 3, w_router=w_r_pad,
                                           pack=True)
            slot_tok, pos, probs, be, bv, nr = _moe_plan(logits[:, :E], E, rows)
            be = be + j * E
            nr = nr.reshape(1)
            x_sorted = gather_rows(hff, slot_tok, rows=rows)
            n_moe = w_exp_gate.shape[0]
            hidden = glu_up(x_sorted, w_exp_gate.reshape(n_moe * E, D, F_ff),
                            w_exp_up.reshape(n_moe * E, D, F_ff), be, bv, nr, tm=rows, tn=tn_ff, packed=True)
            ys = glu_down(hidden, w_exp_down.reshape(n_moe * E, F_ff, D), be, bv, nr,
                          tm=rows, tn=tn_dn, tk=tk_ff)
            f, moe = None, (ys, pos, probs)
        xs = [x]

        if l + 1 < depth:
            x, hmix = post_residual(xs, f, vec(norm_ffn_post[l]), mod, seq_te, tm=te, gate=m0 + 5, moe=moe,
                                    gpre=vec(norm_mix_pre[l + 1]), sc=m0 + 7, sh=m0 + 6)
            xs = [x]
        else:
            y_prompt, y_sample = post_residual(xs, f, vec(norm_ffn_post[l]), mod, seq_te, tm=te, gate=m0 + 5,
                                               moe=moe, out_rows=(T0, T1))

    return (y_prompt.reshape(B, S, D), y_sample.reshape(DB, DS, D))
```

```python
import functools

import numpy as np
import jax
import jax.numpy as jnp
from jax import lax
from jax.experimental import pallas as pl
from jax.experimental.pallas import tpu as pltpu

F32 = jnp.float32
BF16 = jnp.bfloat16

EPS = 1e-6
QK_NOPE_DIM = 128
QK_ROPE_DIM = 64
V_HEAD_DIM = 128
ROPE_THETA = 10000.0
RG_C = 8.0
LOG2_E = 1.4426950408889634
TOP_K = 2
LANES = 128
VMEM_LIMIT = 56 * 1024 * 1024
MOE_ROWS = 1024
MOE_SUB = 512
MOE_PART = 256


def _cparams(*sem):
    return pltpu.CompilerParams(dimension_semantics=sem, vmem_limit_bytes=VMEM_LIMIT)


def _sigmoid(x):
    return 1.0 / (1.0 + jnp.exp(-x))


def _rms(x, g):
    return x * lax.rsqrt(jnp.mean(x * x, axis=-1, keepdims=True) + EPS) * g


def _rope_groups(acc, cs):
    m, n = acc.shape
    reps = n // LANES
    t = acc * (jnp.tile(cs, (1, reps)) if reps > 1 else cs)
    low = lax.broadcasted_iota(jnp.int32, (m, n), 1) % LANES < QK_ROPE_DIM
    outs = []
    for r in range(reps):
        tr = t[:, r * LANES:(r + 1) * LANES]
        outs.append(tr + pltpu.roll(tr, QK_ROPE_DIM, 1))
    y = outs[0] if reps == 1 else jnp.concatenate(outs, axis=1)
    return jnp.where(low, y, 0.0)


def _fk_kernel(pos_ref, *refs, n_in, prologue, has_bias, epilogue):
    del pos_ref
    refs = list(refs)
    xs = [refs.pop(0) for _ in range(n_in)]
    gs = [refs.pop(0) for _ in range(n_in)] if prologue == "rms" else []
    ws = [refs.pop(0) for _ in range(n_in)]
    bias = refs.pop(0) if has_bias else None
    cs = refs.pop(0) if epilogue is not None else None
    out = refs.pop(0)
    xbs = refs

    if prologue is not None:
        @pl.when(pl.program_id(1) == 0)
        def _():
            for i in range(n_in):
                x = xs[i][...].astype(F32)
                if prologue == "rms":
                    x = _rms(x, gs[i][...])
                else:
                    x = x * _sigmoid(x)
                xbs[i][...] = x.astype(BF16)
        lhs = [xb[...] for xb in xbs]
    else:
        lhs = [x[...] for x in xs]

    acc = None
    for i in range(n_in):
        part = jnp.dot(lhs[i], ws[i][...].astype(BF16), preferred_element_type=F32)
        acc = part if acc is None else acc + part
    if has_bias:
        acc = acc + bias[...]
    if epilogue == "rope":
        out[...] = _rope_groups(acc, cs[...]).astype(out.dtype)
    else:
        out[...] = acc.astype(out.dtype)


def fk_matmul(name, xs, ws, *, tm, tn, out_dtype, x_cols=None, gs=None, prologue=None, bias=None,
              w_lead=None, w_rows=None, n_cols=None, epilogue=None, cs=None, tile_pos=None):
    n_in = len(xs)
    M = xs[0].shape[0]
    N = ws[0].shape[-1] if n_cols is None else n_cols
    assert M % tm == 0 and N % tn == 0
    grid = (M // tm, N // tn)
    if tile_pos is None:
        tile_pos = jnp.zeros((grid[0],), jnp.int32)
    if x_cols is None:
        x_cols = [(x.shape[1], 0) for x in xs]

    in_specs, args = [], []
    for x, (kw, cb) in zip(xs, x_cols):
        in_specs.append(pl.BlockSpec((tm, kw), lambda i, j, p, cb=cb: (i, cb)))
        args.append(x)
    if prologue == "rms":
        for g in gs:
            in_specs.append(pl.BlockSpec((1, g.shape[1]), lambda i, j, p: (0, 0)))
            args.append(g)
    for idx, ((kw, _), w) in enumerate(zip(x_cols, ws)):
        if w.ndim == 3:
            rb = 0 if w_rows is None else w_rows[idx]
            in_specs.append(pl.BlockSpec((None, kw, tn), lambda i, j, p, rb=rb: (w_lead, rb, j)))
        else:
            assert w.shape[0] == kw
            in_specs.append(pl.BlockSpec((kw, tn), lambda i, j, p: (0, j)))
        args.append(w)
    if bias is not None:
        in_specs.append(pl.BlockSpec((1, tn), lambda i, j, p: (0, j)))
        args.append(bias)
    if epilogue is not None:
        in_specs.append(pl.BlockSpec((tm, LANES), lambda i, j, p: (p[i], 0)))
        args.append(cs)

    scratch = []
    if prologue is not None:
        scratch = [pltpu.VMEM((tm, kw), BF16) for kw, _ in x_cols]

    kern = functools.partial(_fk_kernel, n_in=n_in, prologue=prologue, has_bias=bias is not None,
                             epilogue=epilogue)
    return pl.pallas_call(
        kern,
        grid_spec=pltpu.PrefetchScalarGridSpec(
            num_scalar_prefetch=1, grid=grid, in_specs=in_specs,
            out_specs=pl.BlockSpec((tm, tn), lambda i, j, p: (i, j)),
            scratch_shapes=scratch),
        out_shape=jax.ShapeDtypeStruct((M, N), out_dtype),
        compiler_params=_cparams("parallel", "arbitrary"),
        name=name,
    )(tile_pos, *args)


def _mod_row(mod_ref, chunk, s):
    return mod_ref[chunk, pl.ds(s, 1), :]


def _split_rows_specs(parts, tm, width):
    specs, start = [], 0
    for p in parts:
        n = p.shape[0] // tm
        specs.append(pl.BlockSpec((tm, width), lambda i, *_, start=start, n=n: (jnp.clip(i - start, 0, n - 1), 0)))
        start += n
    return specs


def _prenorm_kernel(seq_ref, *refs, sc, sh, n_parts, part_tiles):
    xs = refs[:n_parts]
    g_ref, mod_ref, h_ref = refs[n_parts:]
    i = pl.program_id(0)
    s = seq_ref[i]
    x = _select_part(xs, part_tiles, i)
    y = _rms(x, g_ref[...])
    h_ref[...] = (y * (1.0 + _mod_row(mod_ref, sc, s)) + _mod_row(mod_ref, sh, s)).astype(h_ref.dtype)


def _select_part(refs, part_tiles, i):
    x = refs[-1][...]
    end = sum(part_tiles[:-1])
    for k in range(len(refs) - 2, -1, -1):
        x = jnp.where(i < end, refs[k][...], x)
        end -= part_tiles[k]
    return x


def prenorm(xs, g, mod, seq_of_tile, *, tm, sc, sh):
    D = xs[0].shape[1]
    T = sum(x.shape[0] for x in xs)
    part_tiles = tuple(x.shape[0] // tm for x in xs)
    return pl.pallas_call(
        functools.partial(_prenorm_kernel, sc=sc, sh=sh, n_parts=len(xs), part_tiles=part_tiles),
        grid_spec=pltpu.PrefetchScalarGridSpec(
            num_scalar_prefetch=1, grid=(T // tm,),
            in_specs=_split_rows_specs(xs, tm, D) + [
                pl.BlockSpec((1, D), lambda i, s: (0, 0)),
                pl.BlockSpec(mod.shape, lambda i, s: (0, 0, 0))],
            out_specs=pl.BlockSpec((tm, D), lambda i, s: (i, 0))),
        out_shape=jax.ShapeDtypeStruct((T, D), BF16),
        compiler_params=_cparams("parallel"),
        name="prenorm",
    )(seq_of_tile, *xs, g, mod)


def _pack_bf16_pairs(h):
    n = h.shape[1] // 2
    lo = lax.bitcast_convert_type(h[:, :n].astype(BF16).astype(F32), jnp.uint32)
    hi = lax.bitcast_convert_type(h[:, n:].astype(BF16).astype(F32), jnp.uint32)
    return (lo >> 16) | (hi & jnp.uint32(0xFFFF0000))


def _unpack_bf16_pairs(w):
    lo = lax.bitcast_convert_type(w << 16, F32).astype(BF16)
    hi = lax.bitcast_convert_type(w & jnp.uint32(0xFFFF0000), F32).astype(BF16)
    return lo, hi


def _row_dma(src_hbm, row, dst, sem):
    return pltpu.make_async_copy(src_hbm.at[pl.ds(row, 1), :], dst, sem)


def _post_kernel(seq_ref, *refs, gate, sc, sh, n_x, x_tiles, moe, has_next, pack, has_router, out_tiles):
    refs = list(refs)
    xs = [refs.pop(0) for _ in range(n_x)]
    if moe:
        probs_ref, pos_ref, pos_next_ref, ys_hbm = [refs.pop(0) for _ in range(4)]
    else:
        y_ref = refs.pop(0)
    gpost_ref, mod_ref = refs.pop(0), refs.pop(0)
    gpre_ref = refs.pop(0) if has_next else None
    wr_ref = refs.pop(0) if has_router else None
    xo_refs = [refs.pop(0) for _ in out_tiles]
    h_ref = refs.pop(0) if has_next else None
    lg_ref = refs.pop(0) if has_router else None
    i = pl.program_id(0)
    s = seq_ref[i]
    tm = xs[0].shape[0]

    if moe:
        buf, sem = refs
        slot = lax.rem(i, 2)

        def fetch(pos, slot_):
            def body(t, c):
                for k in range(TOP_K):
                    _row_dma(ys_hbm, pos[0, 0, TOP_K * t + k], buf.at[slot_, k, pl.ds(t, 1), :],
                             sem.at[slot_]).start()
                return c
            lax.fori_loop(0, tm, body, 0, unroll=8)

        @pl.when(i == 0)
        def _():
            fetch(pos_ref, 0)

        @pl.when(i + 1 < pl.num_programs(0))
        def _():
            fetch(pos_next_ref, 1 - slot)

        def wait(t, c):
            for k in range(TOP_K):
                _row_dma(ys_hbm, 0, buf.at[slot, k, pl.ds(t, 1), :], sem.at[slot]).wait()
            return c
        lax.fori_loop(0, tm, wait, 0, unroll=8)
        y = probs_ref[:, 0:1] * buf[slot, 0]
        for k in range(1, TOP_K):
            y = y + probs_ref[:, k:k + 1] * buf[slot, k]
    else:
        y = y_ref[...]

    xn = _select_part(xs, x_tiles, i) + _mod_row(mod_ref, gate, s) * _rms(y, gpost_ref[...])
    start = 0
    for xo_ref, n in zip(xo_refs, out_tiles):
        if len(xo_refs) == 1:
            xo_ref[...] = xn
        else:
            @pl.when(jnp.logical_and(i >= start, i < start + n))
            def _(xo_ref=xo_ref):
                xo_ref[...] = xn
        start += n
    if has_next:
        h = _rms(xn, gpre_ref[...]) * (1.0 + _mod_row(mod_ref, sc, s)) + _mod_row(mod_ref, sh, s)
        h_ref[...] = _pack_bf16_pairs(h) if pack else h.astype(h_ref.dtype)
        if has_router:
            lg_ref[...] = jnp.dot(h, wr_ref[...], preferred_element_type=F32,
                                  precision=lax.Precision.HIGHEST)


def post_residual(xs, y, gpost, mod, seq_of_tile, *, tm, gate, gpre=None, sc=None, sh=None, w_router=None,
                  pack=False, moe=None, out_rows=None):
    D = xs[0].shape[1]
    T = sum(x.shape[0] for x in xs)
    nt = T // tm
    has_next = gpre is not None
    has_router = w_router is not None
    row = pl.BlockSpec((tm, D), lambda i, s: (i, 0))
    vec = pl.BlockSpec((1, D), lambda i, s: (0, 0))
    in_specs = _split_rows_specs(xs, tm, D)
    args = list(xs)
    scratch = []
    if moe is not None:
        ys, pos, probs = moe
        pos3 = pos.reshape(nt, 1, TOP_K * tm)
        in_specs += [
            pl.BlockSpec((tm, TOP_K), lambda i, s: (i, 0)),
            pl.BlockSpec((1, 1, TOP_K * tm), lambda i, s: (i, 0, 0), memory_space=pltpu.SMEM),
            pl.BlockSpec((1, 1, TOP_K * tm), lambda i, s: (jnp.minimum(i + 1, nt - 1), 0, 0),
                         memory_space=pltpu.SMEM),
            pl.BlockSpec(memory_space=pl.ANY),
        ]
        args += [probs, pos3, pos3, ys]
        scratch = [pltpu.VMEM((2, TOP_K, tm, D), F32), pltpu.SemaphoreType.DMA((2,))]
    else:
        in_specs.append(row)
        args.append(y)
    in_specs += [vec, pl.BlockSpec(mod.shape, lambda i, s: (0, 0, 0))]
    args += [gpost, mod]
    if has_next:
        in_specs.append(vec)
        args.append(gpre)
    if has_router:
        in_specs.append(pl.BlockSpec(w_router.shape, lambda i, s: (0, 0)))
        args.append(w_router)

    if out_rows is None:
        out_rows = (T,)
    out_tiles = tuple(r // tm for r in out_rows)
    outs = [jax.ShapeDtypeStruct((r, D), F32) for r in out_rows]
    out_shape = list(outs)
    out_specs = _split_rows_specs(outs, tm, D) if len(outs) > 1 else [row]
    if has_next:
        if pack:
            out_shape.append(jax.ShapeDtypeStruct((T, D // 2), jnp.uint32))
            out_specs.append(pl.BlockSpec((tm, D // 2), lambda i, s: (i, 0)))
        else:
            out_shape.append(jax.ShapeDtypeStruct((T, D), BF16))
            out_specs.append(row)
    if has_router:
        out_shape.append(jax.ShapeDtypeStruct((T, w_router.shape[1]), F32))
        out_specs.append(pl.BlockSpec((tm, w_router.shape[1]), lambda i, s: (i, 0)))
    kern = functools.partial(
        _post_kernel, gate=gate, sc=sc, sh=sh, n_x=len(xs), x_tiles=tuple(x.shape[0] // tm for x in xs),
        moe=moe is not None, has_next=has_next, pack=pack, has_router=has_router, out_tiles=out_tiles)
    return pl.pallas_call(
        kern,
        grid_spec=pltpu.PrefetchScalarGridSpec(
            num_scalar_prefetch=1, grid=(nt,), in_specs=in_specs, out_specs=out_specs,
            scratch_shapes=scratch),
        out_shape=out_shape,
        compiler_params=_cparams("arbitrary"),
        name="post_residual",
    )(seq_of_tile, *args)


def _attn_kernel(qn_ref, qr_ref, kn_ref, kr_ref, v_ref, *rest, scale, ck):
    o_ref = rest[-1]
    tq = qn_ref.shape[0]
    S = kn_ref.shape[0]
    q = jnp.concatenate([qn_ref[...], qr_ref[...]], axis=1)
    c2 = scale * LOG2_E
    m = jnp.full((tq, 1), -jnp.inf, F32)
    lpart = jnp.zeros((tq, LANES), F32)
    acc = jnp.zeros((tq, V_HEAD_DIM), F32)
    for c in range(S // ck):
        rows = slice(c * ck, (c + 1) * ck)
        k = jnp.concatenate([kn_ref[rows, :], kr_ref[rows, :]], axis=1)
        s = lax.dot_general(q, k, (((1,), (1,)), ((), ())), preferred_element_type=F32)
        mpart = s[:, :LANES]
        for g in range(1, ck // LANES):
            mpart = jnp.maximum(mpart, s[:, g * LANES:(g + 1) * LANES])
        m_new = jnp.maximum(m, jnp.max(mpart, axis=1, keepdims=True))
        alpha = jnp.exp2((m - m_new) * c2)
        p = jnp.exp2(s * c2 - m_new * c2)
        psum = p[:, :LANES]
        for g in range(1, ck // LANES):
            psum = psum + p[:, g * LANES:(g + 1) * LANES]
        lpart = alpha * lpart + psum
        acc = alpha * acc + jnp.dot(p.astype(BF16), v_ref[rows, :], preferred_element_type=F32)
        m = m_new
    l = jnp.sum(lpart, axis=1, keepdims=True)
    o_ref[...] = (acc / l).astype(o_ref.dtype)


def attention(qn, qr, kv, kr, prev, *, n_heads, n_rows, batch, seq, row0, tq, scale, name):
    assert row0 % seq == 0 and seq % tq == 0
    b0 = row0 // seq
    nq = seq // tq
    q0 = row0 // tq
    H = n_heads
    ck = min(1024, seq)
    in_specs = [
        pl.BlockSpec((tq, LANES), lambda b, h, i: (q0 + b * nq + i, h)),
        pl.BlockSpec((tq, LANES), lambda b, h, i: (q0 + b * nq + i, h)),
        pl.BlockSpec((seq, LANES), lambda b, h, i: (b0 + b, h)),
        pl.BlockSpec((seq, LANES), lambda b, h, i: (b0 + b, 0)),
        pl.BlockSpec((seq, LANES), lambda b, h, i: (b0 + b, H + h)),
    ]
    args = [qn, qr, kv, kr, kv]
    aliases = {}
    if prev is not None:
        in_specs.append(pl.BlockSpec(memory_space=pl.ANY))
        args.append(prev)
        aliases = {5: 0}
    return pl.pallas_call(
        functools.partial(_attn_kernel, scale=scale, ck=ck),
        grid=(batch, H, nq),
        in_specs=in_specs,
        out_specs=pl.BlockSpec((tq, LANES), lambda b, h, i: (q0 + b * nq + i, h)),
        out_shape=jax.ShapeDtypeStruct((n_rows, H * V_HEAD_DIM), BF16),
        input_output_aliases=aliases,
        compiler_params=_cparams("parallel", "parallel", "arbitrary"),
        name=name,
    )(*args)


def _tile_scan(a_ref, b_ref, n, row0, n_tiles, reverse):
    js = list(range(8))[::-1] if reverse else list(range(8))
    slab = lambda j: (n, pl.ds(row0 + j, n_tiles, stride=8), slice(None))
    A = a_ref[slab(js[0])]
    B = b_ref[slab(js[0])]
    for j in js[1:]:
        aj = a_ref[slab(j)]
        B = aj * B + b_ref[slab(j)]
        A = aj * A
        a_ref[slab(j)] = A
        b_ref[slab(j)] = B


def _rglru_kernel(rx_ref, rg_ref, rp_ref, w_ref, *rest, tt):
    o_ref, xpad, a0, b0, a1, b1 = rest[-6:]
    S, tc = rx_ref.shape
    nb = tc // LANES
    zeros8 = jnp.zeros((8, tc), F32)
    xpad[0:8, :] = zeros8
    xpad[8 + S:16 + S, :] = zeros8
    xpad[8:8 + S, :] = rx_ref[...]
    rp = rp_ref[...]
    cw = [rp[j:j + 1, :] for j in range(4)]
    cb = rp[4:5, :]
    hb_a = (rp[5:6, :], rp[6:7, :])
    hb_i = (rp[7:8, :], rp[8:9, :])
    e2 = tuple((-0.5 * RG_C * LOG2_E) * (jnp.maximum(-lam, 0.0) + jnp.log1p(jnp.exp(-jnp.abs(lam))))
               for lam in (rp[9:10, :], rp[10:11, :]))
    ab = ((a0, b0), (a1, b1))

    def chunk(c, carry):
        c0 = pl.multiple_of(c * tt, tt)
        xw = xpad[pl.ds(c0, tt + 16), :]
        xc = (cw[0] * xw[6:6 + tt] + cw[1] * xw[7:7 + tt] + cw[2] * xw[8:8 + tt] + cw[3] * xw[9:9 + tt]) + cb
        for n in range(nb):
            sl = slice(n * LANES, (n + 1) * LANES)
            xcn = xc[:, sl]
            z = jnp.dot(xcn.astype(BF16), w_ref[n].astype(BF16), preferred_element_type=F32)
            xh = 0.5 * xcn
            for d in range(2):
                th_a = jnp.tanh(z[:, d * LANES:(d + 1) * LANES] + hb_a[d][:, sl])
                th_i = jnp.tanh(z[:, (2 + d) * LANES:(3 + d) * LANES] + hb_i[d][:, sl])
                a = jnp.exp2(e2[d][:, sl] * (1.0 + th_a))
                u = 1.0 - a * a
                gain = u * lax.rsqrt(jnp.maximum(u, 1e-30))
                ab[d][0][n, pl.ds(c0, tt), :] = a
                ab[d][1][n, pl.ds(c0, tt), :] = gain * (1.0 + th_i) * xh
        for n in range(nb):
            for d in range(2):
                _tile_scan(ab[d][0], ab[d][1], n, c0, tt // 8, reverse=(d == 1))
        return carry

    lax.fori_loop(0, S // tt, chunk, 0)

    def scan(t, hs):
        rf = pl.multiple_of(t * 8, 8)
        rb = pl.multiple_of(S - 8 - t * 8, 8)
        out = []
        for n in range(nb):
            hf, hb = hs[2 * n], hs[2 * n + 1]
            h = a0[n, pl.ds(rf, 8), :] * hf + b0[n, pl.ds(rf, 8), :]
            b0[n, pl.ds(rf, 8), :] = h
            out.append(jnp.broadcast_to(h[7:8, :], (8, LANES)))
            h = a1[n, pl.ds(rb, 8), :] * hb + b1[n, pl.ds(rb, 8), :]
            b1[n, pl.ds(rb, 8), :] = h
            out.append(jnp.broadcast_to(h[0:1, :], (8, LANES)))
        return tuple(out)

    h0 = jnp.zeros((8, LANES), F32)
    lax.fori_loop(0, S // 8, scan, (h0,) * (2 * nb), unroll=4)

    def gate(c, carry):
        c0 = pl.multiple_of(c * tt, tt)
        g = rg_ref[pl.ds(c0, tt), :]
        gelu = 0.5 * g * (1.0 + jnp.tanh(0.7978845608028654 * (g + 0.044715 * (g * g * g))))
        hsum = jnp.concatenate([b0[n, pl.ds(c0, tt), :] + b1[n, pl.ds(c0, tt), :] for n in range(nb)], axis=1)
        o_ref[pl.ds(c0, tt), :] = (gelu * hsum).astype(o_ref.dtype)
        return carry

    lax.fori_loop(0, S // tt, gate, 0)


def rglru(r, rp, w_cat, prev, *, n_rows, batch, seq, row0, tc, name):
    C = r.shape[1] // 2
    assert row0 % seq == 0 and C % tc == 0
    b0 = row0 // seq
    nct = C // tc
    tt = min(256, seq)
    in_specs = [
        pl.BlockSpec((seq, tc), lambda b, c: (b0 + b, c)),
        pl.BlockSpec((seq, tc), lambda b, c: (b0 + b, nct + c)),
        pl.BlockSpec((16, tc), lambda b, c: (0, c)),
        pl.BlockSpec((tc // LANES, LANES, 4 * LANES), lambda b, c: (c, 0, 0)),
    ]
    args = [r, r, rp, w_cat]
    aliases = {}
    if prev is not None:
        in_specs.append(pl.BlockSpec(memory_space=pl.ANY))
        args.append(prev)
        aliases = {4: 0}
    return pl.pallas_call(
        functools.partial(_rglru_kernel, tt=tt),
        grid=(batch, nct),
        in_specs=in_specs,
        out_specs=pl.BlockSpec((seq, tc), lambda b, c: (b0 + b, c)),
        out_shape=jax.ShapeDtypeStruct((n_rows, C), BF16),
        scratch_shapes=[pltpu.VMEM((seq + 16, tc), F32)] + [pltpu.VMEM((tc // LANES, seq, LANES), F32)] * 4,
        input_output_aliases=aliases,
        compiler_params=_cparams("parallel", "parallel"),
        name=name,
    )(*args)


def _gather_rows_kernel(idx_ref, x_hbm, o_ref, sem, *, rows):
    def start(r, c):
        _row_dma(x_hbm, idx_ref[0, 0, r], o_ref.at[pl.ds(r, 1), :], sem).start()
        return c
    lax.fori_loop(0, rows, start, 0, unroll=8)

    def wait(r, c):
        _row_dma(x_hbm, 0, o_ref.at[pl.ds(r, 1), :], sem).wait()
        return c
    lax.fori_loop(0, rows, wait, 0, unroll=8)


def gather_rows(x, idx, *, rows):
    P = idx.shape[0]
    assert P % rows == 0
    return pl.pallas_call(
        functools.partial(_gather_rows_kernel, rows=rows),
        grid=(P // rows,),
        in_specs=[pl.BlockSpec((1, 1, rows), lambda i: (i, 0, 0), memory_space=pltpu.SMEM),
                  pl.BlockSpec(memory_space=pl.ANY)],
        out_specs=pl.BlockSpec((rows, x.shape[1]), lambda i: (i, 0)),
        out_shape=jax.ShapeDtypeStruct((P, x.shape[1]), x.dtype),
        scratch_shapes=[pltpu.SemaphoreType.DMA(())],
        compiler_params=_cparams("arbitrary"),
        name="gather_rows",
    )(idx.reshape(P // rows, 1, rows), x)


def _row_variants(valid, tm, part):
    return [(q, jnp.logical_and(valid > q - part, valid <= q)) for q in range(part, tm + 1, part)]


def _glu_kernel(be_ref, bv_ref, nr_ref, nxt_ref, x_ref, wg_hbm, wu_hbm, o_ref, land, wb, sem, *,
                part, half, tn, packed):
    j, i = pl.program_id(0), pl.program_id(1)
    last = nr_ref[0] - 1
    ie = jnp.minimum(i, last)
    e = be_ref[ie]
    changed = jnp.logical_or(i == 0, e != be_ref[jnp.clip(i - 1, 0, last)])

    def weight_copies(e_, j_):
        col = pl.multiple_of(j_ * tn, tn)
        return [pltpu.make_async_copy(w.at[e_, :, pl.ds(col, tn)], land.at[k], sem.at[k])
                for k, w in enumerate((wg_hbm, wu_hbm))]

    @pl.when(changed)
    def _():
        @pl.when(jnp.logical_and(j == 0, i == 0))
        def _():
            for c in weight_copies(e, j):
                c.start()
        for c in weight_copies(e, j):
            c.wait()
        for k in range(2):
            wb[k] = land[k].astype(BF16)
        nxt = nxt_ref[ie]

        @pl.when(nxt >= 0)
        def _():
            for c in weight_copies(nxt, j):
                c.start()

        @pl.when(jnp.logical_and(nxt < 0, j + 1 < pl.num_programs(0)))
        def _():
            for c in weight_copies(be_ref[0], j + 1):
                c.start()

    valid = bv_ref[i]
    for r0 in range(0, x_ref.shape[0], half):
        v_piece = jnp.clip(valid - r0, 0, half)

        @pl.when(v_piece == 0)
        def _(r0=r0):
            o_ref[r0:r0 + half, :] = jnp.zeros((half, o_ref.shape[1]), o_ref.dtype)

        for q, pred in _row_variants(v_piece, half, part):
            @pl.when(pred)
            def _(q=q, r0=r0):
                if packed:
                    kh = wb.shape[1] // 2
                    x_lo, x_hi = _unpack_bf16_pairs(x_ref[r0:r0 + q, :])
                    g, u = [jnp.dot(x_lo, wb[k, :kh, :], preferred_element_type=F32)
                            + jnp.dot(x_hi, wb[k, kh:, :], preferred_element_type=F32) for k in range(2)]
                else:
                    x = x_ref[r0:r0 + q, :]
                    g, u = [jnp.dot(x, wb[k], preferred_element_type=F32) for k in range(2)]
                o_ref[r0:r0 + q, :] = (g * _sigmoid(g) * u).astype(o_ref.dtype)
                if q < half:
                    o_ref[r0 + q:r0 + half, :] = jnp.zeros((half - q, o_ref.shape[1]), o_ref.dtype)


def glu_up(x, wg, wu, block_e, block_valid, n_real, *, tm, tn, packed=False):
    P = x.shape[0]
    _, D, F = wg.shape
    nb = P // tm
    be_c = block_e[jnp.minimum(jnp.arange(nb), n_real[0] - 1)]
    nxt_i = jnp.searchsorted(be_c, be_c, side="right")
    nxt = jnp.where(nxt_i < nb, be_c[jnp.minimum(nxt_i, nb - 1)], -1).astype(jnp.int32)

    def row(i, nr):
        return jnp.minimum(i, nr[0] - 1)

    return pl.pallas_call(
        functools.partial(_glu_kernel, part=min(MOE_PART, tm), half=min(MOE_SUB, tm), tn=tn, packed=packed),
        grid_spec=pltpu.PrefetchScalarGridSpec(
            num_scalar_prefetch=4, grid=(F // tn, nb),
            in_specs=[
                pl.BlockSpec((tm, x.shape[1]), lambda j, i, be, bv, nr, nx: (row(i, nr), 0)),
                pl.BlockSpec(memory_space=pl.ANY),
                pl.BlockSpec(memory_space=pl.ANY),
            ],
            out_specs=pl.BlockSpec((tm, tn), lambda j, i, be, bv, nr, nx: (i, j)),
            scratch_shapes=[pltpu.VMEM((2, D, tn), F32), pltpu.VMEM((2, D, tn), BF16),
                            pltpu.SemaphoreType.DMA((2,))]),
        out_shape=jax.ShapeDtypeStruct((P, F), BF16),
        compiler_params=_cparams("arbitrary", "arbitrary"),
        name="glu_up",
    )(block_e, block_valid, n_real, nxt, x, wg, wu)


def _down_kernel(be_ref, bv_ref, nr_ref, h_ref, w_ref, o_ref, *, part):
    del be_ref, nr_ref
    valid = bv_ref[pl.program_id(1)]

    @pl.when(pl.program_id(2) == 0)
    def _():
        o_ref[...] = jnp.zeros_like(o_ref)

    for q, pred in _row_variants(valid, h_ref.shape[0], part):
        @pl.when(pred)
        def _(q=q):
            o_ref[:q, :] += jnp.dot(h_ref[:q, :], w_ref[...].astype(BF16), preferred_element_type=F32)


def glu_down(h, wd, block_e, block_valid, n_real, *, tm, tn, tk):
    P, F = h.shape
    D = wd.shape[2]
    nb = P // tm

    def row(i, nr):
        return jnp.minimum(i, nr[0] - 1)

    return pl.pallas_call(
        functools.partial(_down_kernel, part=min(MOE_PART, tm)),
        grid_spec=pltpu.PrefetchScalarGridSpec(
            num_scalar_prefetch=3, grid=(D // tn, nb, F // tk),
            in_specs=[
                pl.BlockSpec((tm, tk), lambda n, i, k, be, bv, nr: (row(i, nr), k)),
                pl.BlockSpec((None, tk, tn), lambda n, i, k, be, bv, nr: (be[row(i, nr)], k, n)),
            ],
            out_specs=pl.BlockSpec((tm, tn), lambda n, i, k, be, bv, nr: (i, n))),
        out_shape=jax.ShapeDtypeStruct((P, D), F32),
        compiler_params=_cparams("arbitrary", "arbitrary", "arbitrary"),
        name="glu_down",
    )(block_e, block_valid, n_real, h, wd)


def _rot_half_cols(w):
    half = QK_ROPE_DIM // 2
    return jnp.concatenate([-w[..., half:], w[..., :half]], axis=-1)


def _rope_table(S):
    inv = 1.0 / (ROPE_THETA ** (jnp.arange(0, QK_ROPE_DIM, 2, dtype=F32) / QK_ROPE_DIM))
    ang = jnp.arange(S, dtype=F32)[:, None] * inv[None, :]
    c, s = jnp.cos(ang), jnp.sin(ang)
    return jnp.concatenate([c, c, s, s], axis=1)


def _tile_maps(seqs, tm):
    seq_of_tile, tile_pos = [], []
    for sid, (row0, length) in enumerate(seqs):
        assert row0 % tm == 0 and length % tm == 0
        for t in range(length // tm):
            seq_of_tile.append(sid)
            tile_pos.append(t)
    return jnp.asarray(np.array(seq_of_tile, np.int32)), jnp.asarray(np.array(tile_pos, np.int32))


def _moe_plan(logits, n_experts, rows):
    T = logits.shape[0]
    A = T * TOP_K
    top_vals, top_idx = lax.top_k(logits, TOP_K)
    probs = jax.nn.softmax(top_vals, axis=-1)
    flat_e = top_idx.reshape(A).astype(jnp.int32)
    flat_tok = jnp.repeat(jnp.arange(T, dtype=jnp.int32), TOP_K)
    order = jnp.argsort(flat_e)
    sorted_e = flat_e[order]
    counts = jnp.bincount(flat_e, length=n_experts).astype(jnp.int32)
    padded = (counts + rows - 1) // rows * rows
    padded_end = jnp.cumsum(padded)
    start = jnp.cumsum(counts) - counts
    start_pad = padded_end - padded
    dest = start_pad[sorted_e] + jnp.arange(A, dtype=jnp.int32) - start[sorted_e]
    n_blocks = (A + n_experts * (rows - 1) + rows - 1) // rows
    P = n_blocks * rows
    pos = dest[jnp.argsort(order)].reshape(T, TOP_K)
    block_start = jnp.arange(n_blocks, dtype=jnp.int32) * rows
    block_e = jnp.minimum(jnp.searchsorted(padded_end, block_start, side="right"), n_experts - 1).astype(jnp.int32)
    slot = jnp.arange(P, dtype=jnp.int32)
    slot_e = jnp.repeat(block_e, rows)
    k_in = slot - start_pad[slot_e]
    src = jnp.clip(start[slot_e] + k_in, 0, A - 1)
    slot_tok = jnp.where(k_in < counts[slot_e], flat_tok[order][src], 0)
    group_end = start_pad + counts
    block_valid = jnp.clip(group_end[block_e] - block_start, 0, rows)
    block_valid = jnp.where(block_start < padded_end[-1], block_valid, 0).astype(jnp.int32)
    n_real = (padded_end[-1] // rows).astype(jnp.int32)
    return slot_tok, pos, probs, block_e, block_valid, n_real


def kernel(x_prompt, x_sample, c_prompt, c_sample, w_ada, b_ada, norm_mix_pre, norm_mix_post, norm_ffn_pre, norm_ffn_post, w_in, q_norm, w_q_up, kv_norm, w_kv_up, conv_w, conv_b, w_rg_a, b_rg_a, w_rg_i, b_rg_i, rg_lambda, attn_out_norm, rnn_out_norm, w_out, w_ff_gate, w_ff_up, w_ff_down, w_router, w_exp_gate, w_exp_up, w_exp_down):
    B, S, D = x_prompt.shape
    DB, DS, _ = x_sample.shape
    depth = w_in.shape[0]
    T0, T1 = B * S, DB * DS
    T = T0 + T1
    q_rank = q_norm.shape[1]
    kv_rank = kv_norm.shape[1]
    H = w_q_up.shape[2] // (QK_NOPE_DIM + QK_ROPE_DIM)
    C = conv_w.shape[2]
    n_rnn_blocks = w_rg_a.shape[2]
    assert w_rg_a.shape[3] == LANES and C == n_rnn_blocks * LANES
    F_ff = w_ff_gate.shape[2]
    E = w_router.shape[2]
    scale = (QK_NOPE_DIM + QK_ROPE_DIM) ** -0.5

    seqs = [(b * S, S) for b in range(B)] + [(T0 + b * DS, DS) for b in range(DB)]
    s_min = min(S, DS)
    tm = min(1024, s_min)
    te = min(256, s_min)
    _, pos_tm = _tile_maps(seqs, tm)
    seq_te, _ = _tile_maps(seqs, te)
    cs_table = _rope_table(max(S, DS))

    xs = [x_prompt.reshape(T0, D), x_sample.reshape(T1, D)]
    n_seq = B + DB
    c_all = jnp.concatenate([c_prompt, c_sample, jnp.zeros((-n_seq % 16, D), F32)], axis=0)

    def vec(p):
        return p.reshape(1, -1)

    mods = []
    for l in range(depth):
        m = fk_matmul("ada", [c_all], [w_ada], w_lead=l, prologue="silu", bias=vec(b_ada[l]),
                      tm=c_all.shape[0], tn=min(1024, 6 * D), out_dtype=F32)
        mods.append(m.reshape(c_all.shape[0], 6, D).transpose(1, 0, 2))
    mod = jnp.concatenate(mods, axis=0)

    tq = min(1024, s_min)
    rows = min(MOE_ROWS, s_min)
    tn_ff = min(512, F_ff)
    tk_ff = min(1024, F_ff)
    tn_dn = min(2048, D)

    hmix = prenorm(xs, vec(norm_mix_pre[0]), mod, seq_te, tm=te, sc=1, sh=0)

    for l in range(depth):
        m0 = 6 * l
        wl = w_in[l]
        n_lat = q_rank + kv_rank
        w_kr = wl[:, n_lat:n_lat + QK_ROPE_DIM]
        tn_qkv = next(t for t in (512, 256, 128) if n_lat % t == 0)
        assert q_rank % kv_rank == 0
        w_r = wl[:, n_lat + QK_ROPE_DIM:]
        qkv = fk_matmul("qkv_proj", [hmix], [wl[:, :n_lat]], tm=tm, tn=tn_qkv, out_dtype=F32)
        k_rope = fk_matmul("k_rope", [hmix], [jnp.concatenate([w_kr, _rot_half_cols(w_kr)], axis=1)],
                           tm=tm, tn=LANES, out_dtype=BF16, epilogue="rope", cs=cs_table, tile_pos=pos_tm)
        r = fk_matmul("r_proj", [hmix], [w_r], tm=tm, tn=min(512, 2 * C), out_dtype=F32)

        wq = w_q_up[l].reshape(q_rank, H, QK_NOPE_DIM + QK_ROPE_DIM)
        w_qn = wq[:, :, :QK_NOPE_DIM].reshape(q_rank, H * QK_NOPE_DIM)
        wq_r = wq[:, :, QK_NOPE_DIM:]
        w_qr = jnp.concatenate([wq_r, _rot_half_cols(wq_r)], axis=-1).reshape(q_rank, H * LANES)
        wkv = w_kv_up[l].reshape(kv_rank, H, QK_NOPE_DIM + V_HEAD_DIM)
        w_kv = jnp.concatenate([wkv[:, :, :QK_NOPE_DIM].reshape(kv_rank, H * QK_NOPE_DIM),
                                wkv[:, :, QK_NOPE_DIM:].reshape(kv_rank, H * V_HEAD_DIM)], axis=1)
        qg = vec(q_norm[l])
        q_cols = [(q_rank, 0)]
        kv_cols = [(kv_rank, q_rank // kv_rank)]
        qn = fk_matmul("q_nope", [qkv], [w_qn], x_cols=q_cols, gs=[qg], prologue="rms", tm=tm,
                       tn=min(1024, w_qn.shape[1]), out_dtype=BF16)
        qr = fk_matmul("q_rope", [qkv], [w_qr], x_cols=q_cols, gs=[qg], prologue="rms", tm=tm,
                       tn=min(1024, w_qr.shape[1]), out_dtype=BF16, epilogue="rope",
                       cs=cs_table, tile_pos=pos_tm)
        kv = fk_matmul("kv_up", [qkv], [w_kv], x_cols=kv_cols, gs=[vec(kv_norm[l])], prologue="rms", tm=tm,
                       tn=min(1024, w_kv.shape[1]), out_dtype=BF16)

        y_att = jnp.zeros((T, H * V_HEAD_DIM), BF16)
        y_att = attention(qn, qr, kv, k_rope, y_att, n_heads=H, n_rows=T, batch=B, seq=S, row0=0,
                          tq=min(tq, S), scale=scale, name="attn_prompt")
        y_att = attention(qn, qr, kv, k_rope, y_att, n_heads=H, n_rows=T, batch=DB, seq=DS, row0=T0,
                          tq=min(tq, DS), scale=scale, name="attn_sample")

        rp = jnp.concatenate([conv_w[l], conv_b[l][None], 0.5 * b_rg_a[l], 0.5 * b_rg_i[l], rg_lambda[l],
                              jnp.zeros((5, C), F32)], axis=0)
        w_cat = 0.5 * jnp.concatenate([w_rg_a[l, 0], w_rg_a[l, 1], w_rg_i[l, 0], w_rg_i[l, 1]], axis=-1)
        tc = min(256, C)
        y_rnn = jnp.zeros((T, C), BF16)
        y_rnn = rglru(r, rp, w_cat, y_rnn, n_rows=T, batch=B, seq=S, row0=0, tc=tc, name="rglru_prompt")
        y_rnn = rglru(r, rp, w_cat, y_rnn, n_rows=T, batch=DB, seq=DS, row0=T0, tc=tc, name="rglru_sample")

        assert w_out.shape[1] == 2 * y_att.shape[1] and y_rnn.shape[1] == y_att.shape[1]
        y_mix = fk_matmul("out_proj", [y_att, y_rnn], [w_out, w_out], w_lead=l, w_rows=(0, 1),
                          gs=[vec(attn_out_norm[l]), vec(rnn_out_norm[l])], prologue="rms",
                          tm=tm, tn=min(512, D), out_dtype=F32)

        j = l // 2
        moe = None
        if l % 2 == 0:
            x, hff = post_residual(xs, y_mix, vec(norm_mix_post[l]), mod, seq_te, tm=te, gate=m0 + 2,
                                   gpre=vec(norm_ffn_pre[l]), sc=m0 + 4, sh=m0 + 3)
            nb = T // rows
            be = jnp.full((nb,), j, jnp.int32)
            bv = jnp.full((nb,), rows, jnp.int32)
            nr = jnp.full((1,), nb, jnp.int32)
            hidden = glu_up(hff, w_ff_gate, w_ff_up, be, bv, nr, tm=rows, tn=tn_ff)
            f = glu_down(hidden, w_ff_down, be, bv, nr, tm=rows, tn=tn_dn, tk=tk_ff)
        else:
            w_r_pad = jnp.concatenate([w_router[j], jnp.zeros((D, LANES - E), F32)], axis=1)
            x, hff, logits = post_residual(xs, y_mix, vec(norm_mix_post[l]), mod, seq_te, tm=te, gate=m0 + 2,
                                           gpre=vec(norm_ffn_pre[l]), sc=m0 + 4, sh=m0 + 3, w_router=w_r_pad,
                                           pack=True)
            slot_tok, pos, probs, be, bv, nr = _moe_plan(logits[:, :E], E, rows)
            be = be + j * E
            nr = nr.reshape(1)
            x_sorted = gather_rows(hff, slot_tok, rows=rows)
            n_moe = w_exp_gate.shape[0]
            hidden = glu_up(x_sorted, w_exp_gate.reshape(n_moe * E, D, F_ff),
                            w_exp_up.reshape(n_moe * E, D, F_ff), be, bv, nr, tm=rows, tn=tn_ff, packed=True)
            ys = glu_down(hidden, w_exp_down.reshape(n_moe * E, F_ff, D), be, bv, nr,
                          tm=rows, tn=tn_dn, tk=tk_ff)
            f, moe = None, (ys, pos, probs)
        xs = [x]

        if l + 1 < depth:
            x, hmix = post_residual(xs, f, vec(norm_ffn_post[l]), mod, seq_te, tm=te, gate=m0 + 5, moe=moe,
                                    gpre=vec(norm_mix_pre[l + 1]), sc=m0 + 7, sh=m0 + 6)
            xs = [x]
        else:
            y_prompt, y_sample = post_residual(xs, f, vec(norm_ffn_post[l]), mod, seq_te, tm=te, gate=m0 + 5,
                                               moe=moe, out_rows=(T0, T1))

    return (y_prompt.reshape(B, S, D), y_sample.reshape(DB, DS, D))
```

```python
import functools

import numpy as np
import jax
import jax.numpy as jnp
from jax import lax
from jax.experimental import pallas as pl
from jax.experimental.pallas import tpu as pltpu

F32 = jnp.float32
BF16 = jnp.bfloat16

EPS = 1e-6
QK_NOPE_DIM = 128
QK_ROPE_DIM = 64
V_HEAD_DIM = 128
ROPE_THETA = 10000.0
RG_C = 8.0
LOG2_E = 1.4426950408889634
TOP_K = 2
LANES = 128
VMEM_LIMIT = 56 * 1024 * 1024
MOE_ROWS = 1024
MOE_SUB = 512
MOE_PART = 256


def _cparams(*sem):
    return pltpu.CompilerParams(dimension_semantics=sem, vmem_limit_bytes=VMEM_LIMIT)


def _sigmoid(x):
    return 1.0 / (1.0 + jnp.exp(-x))


def _rms(x, g):
    return x * lax.rsqrt(jnp.mean(x * x, axis=-1, keepdims=True) + EPS) * g


def _rope_groups(acc, cs):
    m, n = acc.shape
    reps = n // LANES
    t = acc * (jnp.tile(cs, (1, reps)) if reps > 1 else cs)
    low = lax.broadcasted_iota(jnp.int32, (m, n), 1) % LANES < QK_ROPE_DIM
    outs = []
    for r in range(reps):
        tr = t[:, r * LANES:(r + 1) * LANES]
        outs.append(tr + pltpu.roll(tr, QK_ROPE_DIM, 1))
    y = outs[0] if reps == 1 else jnp.concatenate(outs, axis=1)
    return jnp.where(low, y, 0.0)


def _fk_kernel(pos_ref, *refs, n_in, prologue, has_bias, epilogue, rope_from):
    del pos_ref
    refs = list(refs)
    xs = [refs.pop(0) for _ in range(n_in)]
    gs = [refs.pop(0) for _ in range(n_in)] if prologue == "rms" else []
    ws = [refs.pop(0) for _ in range(n_in)]
    bias = refs.pop(0) if has_bias else None
    cs = refs.pop(0) if epilogue is not None else None
    out = refs.pop(0)
    xbs = refs

    if prologue is not None:
        @pl.when(pl.program_id(1) == 0)
        def _():
            for i in range(n_in):
                x = xs[i][...].astype(F32)
                if prologue == "rms":
                    x = _rms(x, gs[i][...])
                else:
                    x = x * _sigmoid(x)
                xbs[i][...] = x.astype(BF16)
        lhs = [xb[...] for xb in xbs]
    else:
        lhs = [x[...] for x in xs]

    acc = None
    for i in range(n_in):
        part = jnp.dot(lhs[i], ws[i][...].astype(BF16), preferred_element_type=F32)
        acc = part if acc is None else acc + part
    if has_bias:
        acc = acc + bias[...]
    if epilogue == "rope":
        roped = pl.program_id(1) >= rope_from

        @pl.when(roped)
        def _():
            out[...] = _rope_groups(acc, cs[...]).astype(out.dtype)

        @pl.when(jnp.logical_not(roped))
        def _():
            out[...] = acc.astype(out.dtype)
    else:
        out[...] = acc.astype(out.dtype)


def fk_matmul(name, xs, ws, *, tm, tn, out_dtype, x_cols=None, gs=None, prologue=None, bias=None,
              w_lead=None, w_rows=None, w_lead_tiles=None, n_cols=None, epilogue=None, rope_from=0, cs=None,
              tile_pos=None):
    n_in = len(xs)
    M = xs[0].shape[0]
    N = ws[0].shape[-1] if n_cols is None else n_cols
    if w_lead_tiles is not None:
        assert w_lead_tiles * tn == ws[0].shape[-1]
        N = ws[0].shape[0] * ws[0].shape[-1]
    assert M % tm == 0 and N % tn == 0
    grid = (M // tm, N // tn)
    if tile_pos is None:
        tile_pos = jnp.zeros((grid[0],), jnp.int32)
    if x_cols is None:
        x_cols = [(x.shape[1], 0) for x in xs]

    in_specs, args = [], []
    for x, (kw, cb) in zip(xs, x_cols):
        in_specs.append(pl.BlockSpec((tm, kw), lambda i, j, p, cb=cb: (i, cb)))
        args.append(x)
    if prologue == "rms":
        for g in gs:
            in_specs.append(pl.BlockSpec((1, g.shape[1]), lambda i, j, p: (0, 0)))
            args.append(g)
    for idx, ((kw, _), w) in enumerate(zip(x_cols, ws)):
        if w.ndim == 3:
            rb = 0 if w_rows is None else w_rows[idx]
            if w_lead_tiles is None:
                in_specs.append(pl.BlockSpec((None, kw, tn), lambda i, j, p, rb=rb: (w_lead, rb, j)))
            else:
                nt = w_lead_tiles
                in_specs.append(pl.BlockSpec((None, kw, tn), lambda i, j, p, rb=rb: (j // nt, rb, j % nt)))
        else:
            assert w.shape[0] == kw
            in_specs.append(pl.BlockSpec((kw, tn), lambda i, j, p: (0, j)))
        args.append(w)
    if bias is not None:
        in_specs.append(pl.BlockSpec((1, tn), lambda i, j, p: (0, j)))
        args.append(bias)
    if epilogue is not None:
        in_specs.append(pl.BlockSpec((tm, LANES), lambda i, j, p: (p[i], 0)))
        args.append(cs)

    scratch = []
    if prologue is not None:
        scratch = [pltpu.VMEM((tm, kw), BF16) for kw, _ in x_cols]

    kern = functools.partial(_fk_kernel, n_in=n_in, prologue=prologue, has_bias=bias is not None,
                             epilogue=epilogue, rope_from=rope_from)
    return pl.pallas_call(
        kern,
        grid_spec=pltpu.PrefetchScalarGridSpec(
            num_scalar_prefetch=1, grid=grid, in_specs=in_specs,
            out_specs=pl.BlockSpec((tm, tn), lambda i, j, p: (i, j)),
            scratch_shapes=scratch),
        out_shape=jax.ShapeDtypeStruct((M, N), out_dtype),
        compiler_params=_cparams("parallel", "arbitrary"),
        name=name,
    )(tile_pos, *args)


def _mod_row(mod_ref, chunk, s):
    return mod_ref[chunk, pl.ds(s, 1), :]


def _split_rows_specs(parts, tm, width):
    specs, start = [], 0
    for p in parts:
        n = p.shape[0] // tm
        specs.append(pl.BlockSpec((tm, width), lambda i, *_, start=start, n=n: (jnp.clip(i - start, 0, n - 1), 0)))
        start += n
    return specs


def _prenorm_kernel(seq_ref, *refs, sc, sh, n_parts, part_tiles):
    xs = refs[:n_parts]
    g_ref, mod_ref, h_ref = refs[n_parts:]
    i = pl.program_id(0)
    s = seq_ref[i]
    x = _select_part(xs, part_tiles, i)
    y = _rms(x, g_ref[...])
    h_ref[...] = (y * (1.0 + _mod_row(mod_ref, sc, s)) + _mod_row(mod_ref, sh, s)).astype(h_ref.dtype)


def _select_part(refs, part_tiles, i):
    x = refs[-1][...]
    end = sum(part_tiles[:-1])
    for k in range(len(refs) - 2, -1, -1):
        x = jnp.where(i < end, refs[k][...], x)
        end -= part_tiles[k]
    return x


def prenorm(xs, g, mod, seq_of_tile, *, tm, sc, sh):
    D = xs[0].shape[1]
    T = sum(x.shape[0] for x in xs)
    part_tiles = tuple(x.shape[0] // tm for x in xs)
    return pl.pallas_call(
        functools.partial(_prenorm_kernel, sc=sc, sh=sh, n_parts=len(xs), part_tiles=part_tiles),
        grid_spec=pltpu.PrefetchScalarGridSpec(
            num_scalar_prefetch=1, grid=(T // tm,),
            in_specs=_split_rows_specs(xs, tm, D) + [
                pl.BlockSpec((1, D), lambda i, s: (0, 0)),
                pl.BlockSpec(mod.shape, lambda i, s: (0, 0, 0))],
            out_specs=pl.BlockSpec((tm, D), lambda i, s: (i, 0))),
        out_shape=jax.ShapeDtypeStruct((T, D), BF16),
        compiler_params=_cparams("parallel"),
        name="prenorm",
    )(seq_of_tile, *xs, g, mod)


def _pack_bf16_pairs(h):
    n = h.shape[1] // 2
    lo = lax.bitcast_convert_type(h[:, :n].astype(BF16).astype(F32), jnp.uint32)
    hi = lax.bitcast_convert_type(h[:, n:].astype(BF16).astype(F32), jnp.uint32)
    return (lo >> 16) | (hi & jnp.uint32(0xFFFF0000))


def _unpack_bf16_pairs(w):
    lo = lax.bitcast_convert_type(w << 16, F32).astype(BF16)
    hi = lax.bitcast_convert_type(w & jnp.uint32(0xFFFF0000), F32).astype(BF16)
    return lo, hi


def _row_dma(src_hbm, row, dst, sem):
    return pltpu.make_async_copy(src_hbm.at[pl.ds(row, 1), :], dst, sem)


def _post_kernel(seq_ref, *refs, gate, sc, sh, n_x, x_tiles, moe, has_next, pack, has_router, out_tiles):
    refs = list(refs)
    xs = [refs.pop(0) for _ in range(n_x)]
    if moe:
        probs_ref, pos_ref, pos_next_ref, ys_hbm = [refs.pop(0) for _ in range(4)]
    else:
        y_ref = refs.pop(0)
    gpost_ref, mod_ref = refs.pop(0), refs.pop(0)
    gpre_ref = refs.pop(0) if has_next else None
    wr_ref = refs.pop(0) if has_router else None
    xo_refs = [refs.pop(0) for _ in out_tiles]
    h_ref = refs.pop(0) if has_next else None
    lg_ref = refs.pop(0) if has_router else None
    i = pl.program_id(0)
    s = seq_ref[i]
    tm = xs[0].shape[0]

    if moe:
        buf, sem = refs
        slot = lax.rem(i, 2)

        def fetch(pos, slot_):
            def body(t, c):
                for k in range(TOP_K):
                    _row_dma(ys_hbm, pos[0, 0, TOP_K * t + k], buf.at[slot_, k, pl.ds(t, 1), :],
                             sem.at[slot_]).start()
                return c
            lax.fori_loop(0, tm, body, 0, unroll=8)

        @pl.when(i == 0)
        def _():
            fetch(pos_ref, 0)

        @pl.when(i + 1 < pl.num_programs(0))
        def _():
            fetch(pos_next_ref, 1 - slot)

        def wait(t, c):
            for k in range(TOP_K):
                _row_dma(ys_hbm, 0, buf.at[slot, k, pl.ds(t, 1), :], sem.at[slot]).wait()
            return c
        lax.fori_loop(0, tm, wait, 0, unroll=8)
        y = probs_ref[:, 0:1] * buf[slot, 0]
        for k in range(1, TOP_K):
            y = y + probs_ref[:, k:k + 1] * buf[slot, k]
    else:
        y = y_ref[...]

    xn = _select_part(xs, x_tiles, i) + _mod_row(mod_ref, gate, s) * _rms(y, gpost_ref[...])
    start = 0
    for xo_ref, n in zip(xo_refs, out_tiles):
        if len(xo_refs) == 1:
            xo_ref[...] = xn
        else:
            @pl.when(jnp.logical_and(i >= start, i < start + n))
            def _(xo_ref=xo_ref):
                xo_ref[...] = xn
        start += n
    if has_next:
        h = _rms(xn, gpre_ref[...]) * (1.0 + _mod_row(mod_ref, sc, s)) + _mod_row(mod_ref, sh, s)
        h_ref[...] = _pack_bf16_pairs(h) if pack else h.astype(h_ref.dtype)
        if has_router:
            lg_ref[...] = jnp.dot(h, wr_ref[...], preferred_element_type=F32,
                                  precision=lax.Precision.HIGHEST)


def post_residual(xs, y, gpost, mod, seq_of_tile, *, tm, gate, gpre=None, sc=None, sh=None, w_router=None,
                  pack=False, moe=None, out_rows=None):
    D = xs[0].shape[1]
    T = sum(x.shape[0] for x in xs)
    nt = T // tm
    has_next = gpre is not None
    has_router = w_router is not None
    row = pl.BlockSpec((tm, D), lambda i, s: (i, 0))
    vec = pl.BlockSpec((1, D), lambda i, s: (0, 0))
    in_specs = _split_rows_specs(xs, tm, D)
    args = list(xs)
    scratch = []
    if moe is not None:
        ys, pos, probs = moe
        pos3 = pos.reshape(nt, 1, TOP_K * tm)
        in_specs += [
            pl.BlockSpec((tm, TOP_K), lambda i, s: (i, 0)),
            pl.BlockSpec((1, 1, TOP_K * tm), lambda i, s: (i, 0, 0), memory_space=pltpu.SMEM),
            pl.BlockSpec((1, 1, TOP_K * tm), lambda i, s: (jnp.minimum(i + 1, nt - 1), 0, 0),
                         memory_space=pltpu.SMEM),
            pl.BlockSpec(memory_space=pl.ANY),
        ]
        args += [probs, pos3, pos3, ys]
        scratch = [pltpu.VMEM((2, TOP_K, tm, D), F32), pltpu.SemaphoreType.DMA((2,))]
    else:
        in_specs.append(row)
        args.append(y)
    in_specs += [vec, pl.BlockSpec(mod.shape, lambda i, s: (0, 0, 0))]
    args += [gpost, mod]
    if has_next:
        in_specs.append(vec)
        args.append(gpre)
    if has_router:
        in_specs.append(pl.BlockSpec(w_router.shape, lambda i, s: (0, 0)))
        args.append(w_router)

    if out_rows is None:
        out_rows = (T,)
    out_tiles = tuple(r // tm for r in out_rows)
    outs = [jax.ShapeDtypeStruct((r, D), F32) for r in out_rows]
    out_shape = list(outs)
    out_specs = _split_rows_specs(outs, tm, D) if len(outs) > 1 else [row]
    if has_next:
        if pack:
            out_shape.append(jax.ShapeDtypeStruct((T, D // 2), jnp.uint32))
            out_specs.append(pl.BlockSpec((tm, D // 2), lambda i, s: (i, 0)))
        else:
            out_shape.append(jax.ShapeDtypeStruct((T, D), BF16))
            out_specs.append(row)
    if has_router:
        out_shape.append(jax.ShapeDtypeStruct((T, w_router.shape[1]), F32))
        out_specs.append(pl.BlockSpec((tm, w_router.shape[1]), lambda i, s: (i, 0)))
    kern = functools.partial(
        _post_kernel, gate=gate, sc=sc, sh=sh, n_x=len(xs), x_tiles=tuple(x.shape[0] // tm for x in xs),
        moe=moe is not None, has_next=has_next, pack=pack, has_router=has_router, out_tiles=out_tiles)
    return pl.pallas_call(
        kern,
        grid_spec=pltpu.PrefetchScalarGridSpec(
            num_scalar_prefetch=1, grid=(nt,), in_specs=in_specs, out_specs=out_specs,
            scratch_shapes=scratch),
        out_shape=out_shape,
        compiler_params=_cparams("arbitrary"),
        name="post_residual",
    )(seq_of_tile, *args)


def _attn_kernel(qn_ref, qr_ref, kn_ref, kr_ref, v_ref, *rest, scale, ck):
    o_ref = rest[-1]
    tq = qn_ref.shape[0]
    S = kn_ref.shape[0]
    q = jnp.concatenate([qn_ref[...], qr_ref[...]], axis=1)
    c2 = scale * LOG2_E
    m = jnp.full((tq, 1), -jnp.inf, F32)
    lpart = jnp.zeros((tq, LANES), F32)
    acc = jnp.zeros((tq, V_HEAD_DIM), F32)
    for c in range(S // ck):
        rows = slice(c * ck, (c + 1) * ck)
        k = jnp.concatenate([kn_ref[rows, :], kr_ref[rows, :]], axis=1)
        s = lax.dot_general(q, k, (((1,), (1,)), ((), ())), preferred_element_type=F32)
        mpart = s[:, :LANES]
        for g in range(1, ck // LANES):
            mpart = jnp.maximum(mpart, s[:, g * LANES:(g + 1) * LANES])
        m_new = jnp.maximum(m, jnp.max(mpart, axis=1, keepdims=True))
        alpha = jnp.exp2((m - m_new) * c2)
        p = jnp.exp2(s * c2 - m_new * c2)
        psum = p[:, :LANES]
        for g in range(1, ck // LANES):
            psum = psum + p[:, g * LANES:(g + 1) * LANES]
        lpart = alpha * lpart + psum
        acc = alpha * acc + jnp.dot(p.astype(BF16), v_ref[rows, :], preferred_element_type=F32)
        m = m_new
    l = jnp.sum(lpart, axis=1, keepdims=True)
    o_ref[...] = (acc / l).astype(o_ref.dtype)


def attention(q, kv, kr, prev, *, n_heads, n_rows, batch, seq, row0, tq, scale, name):
    assert row0 % seq == 0 and seq % tq == 0
    b0 = row0 // seq
    nq = seq // tq
    q0 = row0 // tq
    H = n_heads
    ck = min(1024, seq)
    in_specs = [
        pl.BlockSpec((tq, LANES), lambda b, h, i: (q0 + b * nq + i, h)),
        pl.BlockSpec((tq, LANES), lambda b, h, i: (q0 + b * nq + i, H + h)),
        pl.BlockSpec((seq, LANES), lambda b, h, i: (b0 + b, h)),
        pl.BlockSpec((seq, LANES), lambda b, h, i: (b0 + b, 0)),
        pl.BlockSpec((seq, LANES), lambda b, h, i: (b0 + b, H + h)),
    ]
    args = [q, q, kv, kr, kv]
    aliases = {}
    if prev is not None:
        in_specs.append(pl.BlockSpec(memory_space=pl.ANY))
        args.append(prev)
        aliases = {5: 0}
    return pl.pallas_call(
        functools.partial(_attn_kernel, scale=scale, ck=ck),
        grid=(batch, H, nq),
        in_specs=in_specs,
        out_specs=pl.BlockSpec((tq, LANES), lambda b, h, i: (q0 + b * nq + i, h)),
        out_shape=jax.ShapeDtypeStruct((n_rows, H * V_HEAD_DIM), BF16),
        input_output_aliases=aliases,
        compiler_params=_cparams("parallel", "parallel", "arbitrary"),
        name=name,
    )(*args)


def _tile_scan(a_ref, b_ref, n, row0, n_tiles, reverse):
    js = list(range(8))[::-1] if reverse else list(range(8))
    slab = lambda j: (n, pl.ds(row0 + j, n_tiles, stride=8), slice(None))
    A = a_ref[slab(js[0])]
    B = b_ref[slab(js[0])]
    for j in js[1:]:
        aj = a_ref[slab(j)]
        B = aj * B + b_ref[slab(j)]
        A = aj * A
        a_ref[slab(j)] = A
        b_ref[slab(j)] = B


def _rglru_kernel(rx_ref, rg_ref, rp_ref, w_ref, *rest, tt):
    o_ref, xpad, a0, b0, a1, b1 = rest[-6:]
    S, tc = rx_ref.shape
    nb = tc // LANES
    zeros8 = jnp.zeros((8, tc), F32)
    xpad[0:8, :] = zeros8
    xpad[8 + S:16 + S, :] = zeros8
    xpad[8:8 + S, :] = rx_ref[...]
    rp = rp_ref[...]
    cw = [rp[j:j + 1, :] for j in range(4)]
    cb = rp[4:5, :]
    hb_a = (rp[5:6, :], rp[6:7, :])
    hb_i = (rp[7:8, :], rp[8:9, :])
    e2 = tuple((-0.5 * RG_C * LOG2_E) * (jnp.maximum(-lam, 0.0) + jnp.log1p(jnp.exp(-jnp.abs(lam))))
               for lam in (rp[9:10, :], rp[10:11, :]))
    ab = ((a0, b0), (a1, b1))

    def chunk(c, carry):
        c0 = pl.multiple_of(c * tt, tt)
        xw = xpad[pl.ds(c0, tt + 16), :]
        xc = (cw[0] * xw[6:6 + tt] + cw[1] * xw[7:7 + tt] + cw[2] * xw[8:8 + tt] + cw[3] * xw[9:9 + tt]) + cb
        for n in range(nb):
            sl = slice(n * LANES, (n + 1) * LANES)
            xcn = xc[:, sl]
            z = jnp.dot(xcn.astype(BF16), w_ref[n].astype(BF16), preferred_element_type=F32)
            xh = 0.5 * xcn
            for d in range(2):
                th_a = jnp.tanh(z[:, d * LANES:(d + 1) * LANES] + hb_a[d][:, sl])
                th_i = jnp.tanh(z[:, (2 + d) * LANES:(3 + d) * LANES] + hb_i[d][:, sl])
                a = jnp.exp2(e2[d][:, sl] * (1.0 + th_a))
                u = 1.0 - a * a
                gain = u * lax.rsqrt(jnp.maximum(u, 1e-30))
                ab[d][0][n, pl.ds(c0, tt), :] = a
                ab[d][1][n, pl.ds(c0, tt), :] = gain * (1.0 + th_i) * xh
        for n in range(nb):
            for d in range(2):
                _tile_scan(ab[d][0], ab[d][1], n, c0, tt // 8, reverse=(d == 1))
        return carry

    lax.fori_loop(0, S // tt, chunk, 0)

    def scan(t, hs):
        rf = pl.multiple_of(t * 8, 8)
        rb = pl.multiple_of(S - 8 - t * 8, 8)
        out = []
        for n in range(nb):
            hf, hb = hs[2 * n], hs[2 * n + 1]
            h = a0[n, pl.ds(rf, 8), :] * hf + b0[n, pl.ds(rf, 8), :]
            b0[n, pl.ds(rf, 8), :] = h
            out.append(jnp.broadcast_to(h[7:8, :], (8, LANES)))
            h = a1[n, pl.ds(rb, 8), :] * hb + b1[n, pl.ds(rb, 8), :]
            b1[n, pl.ds(rb, 8), :] = h
            out.append(jnp.broadcast_to(h[0:1, :], (8, LANES)))
        return tuple(out)

    h0 = jnp.zeros((8, LANES), F32)
    lax.fori_loop(0, S // 8, scan, (h0,) * (2 * nb), unroll=4)

    def gate(c, carry):
        c0 = pl.multiple_of(c * tt, tt)
        g = rg_ref[pl.ds(c0, tt), :]
        gelu = 0.5 * g * (1.0 + jnp.tanh(0.7978845608028654 * (g + 0.044715 * (g * g * g))))
        hsum = jnp.concatenate([b0[n, pl.ds(c0, tt), :] + b1[n, pl.ds(c0, tt), :] for n in range(nb)], axis=1)
        o_ref[pl.ds(c0, tt), :] = (gelu * hsum).astype(o_ref.dtype)
        return carry

    lax.fori_loop(0, S // tt, gate, 0)


def rglru(r, rp, w_cat, prev, *, n_rows, batch, seq, row0, tc, name):
    C = r.shape[1] // 2
    assert row0 % seq == 0 and C % tc == 0
    b0 = row0 // seq
    nct = C // tc
    tt = min(256, seq)
    in_specs = [
        pl.BlockSpec((seq, tc), lambda b, c: (b0 + b, c)),
        pl.BlockSpec((seq, tc), lambda b, c: (b0 + b, nct + c)),
        pl.BlockSpec((16, tc), lambda b, c: (0, c)),
        pl.BlockSpec((tc // LANES, LANES, 4 * LANES), lambda b, c: (c, 0, 0)),
    ]
    args = [r, r, rp, w_cat]
    aliases = {}
    if prev is not None:
        in_specs.append(pl.BlockSpec(memory_space=pl.ANY))
        args.append(prev)
        aliases = {4: 0}
    return pl.pallas_call(
        functools.partial(_rglru_kernel, tt=tt),
        grid=(batch, nct),
        in_specs=in_specs,
        out_specs=pl.BlockSpec((seq, tc), lambda b, c: (b0 + b, c)),
        out_shape=jax.ShapeDtypeStruct((n_rows, C), BF16),
        scratch_shapes=[pltpu.VMEM((seq + 16, tc), F32)] + [pltpu.VMEM((tc // LANES, seq, LANES), F32)] * 4,
        input_output_aliases=aliases,
        compiler_params=_cparams("parallel", "parallel"),
        name=name,
    )(*args)


def _gather_rows_kernel(idx_ref, idx_next_ref, x_hbm, o_ref, buf, sem, *, rows):
    i = pl.program_id(0)
    slot = lax.rem(i, 2)

    def fetch(idx, slot_):
        def body(r, c):
            _row_dma(x_hbm, idx[0, 0, r], buf.at[slot_, pl.ds(r, 1), :], sem.at[slot_]).start()
            return c
        lax.fori_loop(0, rows, body, 0, unroll=8)

    @pl.when(i == 0)
    def _():
        fetch(idx_ref, 0)

    @pl.when(i + 1 < pl.num_programs(0))
    def _():
        fetch(idx_next_ref, 1 - slot)

    def wait(r, c):
        _row_dma(x_hbm, 0, buf.at[slot, pl.ds(r, 1), :], sem.at[slot]).wait()
        return c
    lax.fori_loop(0, rows, wait, 0, unroll=8)
    o_ref[...] = buf[slot]


def gather_rows(x, idx, *, rows):
    P = idx.shape[0]
    assert P % rows == 0
    nb = P // rows
    idx3 = idx.reshape(nb, 1, rows)
    return pl.pallas_call(
        functools.partial(_gather_rows_kernel, rows=rows),
        grid=(nb,),
        in_specs=[pl.BlockSpec((1, 1, rows), lambda i: (i, 0, 0), memory_space=pltpu.SMEM),
                  pl.BlockSpec((1, 1, rows), lambda i: (jnp.minimum(i + 1, nb - 1), 0, 0),
                               memory_space=pltpu.SMEM),
                  pl.BlockSpec(memory_space=pl.ANY)],
        out_specs=pl.BlockSpec((rows, x.shape[1]), lambda i: (i, 0)),
        out_shape=jax.ShapeDtypeStruct((P, x.shape[1]), x.dtype),
        scratch_shapes=[pltpu.VMEM((2, rows, x.shape[1]), x.dtype), pltpu.SemaphoreType.DMA((2,))],
        compiler_params=_cparams("arbitrary"),
        name="gather_rows",
    )(idx3, idx3, x)


def _row_variants(valid, tm, part):
    return [(q, jnp.logical_and(valid > q - part, valid <= q)) for q in range(part, tm + 1, part)]


def _glu_kernel(be_ref, bv_ref, nr_ref, nxt_ref, x_ref, wg_hbm, wu_hbm, o_ref, land, wb, sem, *,
                part, half, tn, packed):
    j, i = pl.program_id(0), pl.program_id(1)
    last = nr_ref[0] - 1
    ie = jnp.minimum(i, last)
    e = be_ref[ie]
    changed = jnp.logical_or(i == 0, e != be_ref[jnp.clip(i - 1, 0, last)])

    def weight_copies(e_, j_):
        col = pl.multiple_of(j_ * tn, tn)
        return [pltpu.make_async_copy(w.at[e_, :, pl.ds(col, tn)], land.at[k], sem.at[k])
                for k, w in enumerate((wg_hbm, wu_hbm))]

    @pl.when(changed)
    def _():
        @pl.when(jnp.logical_and(j == 0, i == 0))
        def _():
            for c in weight_copies(e, j):
                c.start()
        for c in weight_copies(e, j):
            c.wait()
        for k in range(2):
            wb[k] = land[k].astype(BF16)
        nxt = nxt_ref[ie]

        @pl.when(nxt >= 0)
        def _():
            for c in weight_copies(nxt, j):
                c.start()

        @pl.when(jnp.logical_and(nxt < 0, j + 1 < pl.num_programs(0)))
        def _():
            for c in weight_copies(be_ref[0], j + 1):
                c.start()

    valid = bv_ref[i]
    for r0 in range(0, x_ref.shape[0], half):
        v_piece = jnp.clip(valid - r0, 0, half)

        @pl.when(v_piece == 0)
        def _(r0=r0):
            o_ref[r0:r0 + half, :] = jnp.zeros((half, o_ref.shape[1]), o_ref.dtype)

        for q, pred in _row_variants(v_piece, half, part):
            @pl.when(pred)
            def _(q=q, r0=r0):
                if packed:
                    kh = wb.shape[1] // 2
                    x_lo, x_hi = _unpack_bf16_pairs(x_ref[r0:r0 + q, :])
                    g, u = [jnp.dot(x_lo, wb[k, :kh, :], preferred_element_type=F32)
                            + jnp.dot(x_hi, wb[k, kh:, :], preferred_element_type=F32) for k in range(2)]
                else:
                    x = x_ref[r0:r0 + q, :]
                    g, u = [jnp.dot(x, wb[k], preferred_element_type=F32) for k in range(2)]
                o_ref[r0:r0 + q, :] = (g * _sigmoid(g) * u).astype(o_ref.dtype)
                if q < half:
                    o_ref[r0 + q:r0 + half, :] = jnp.zeros((half - q, o_ref.shape[1]), o_ref.dtype)


def glu_up(x, wg, wu, block_e, block_valid, n_real, *, tm, tn, packed=False):
    P = x.shape[0]
    _, D, F = wg.shape
    nb = P // tm
    be_c = block_e[jnp.minimum(jnp.arange(nb), n_real[0] - 1)]
    nxt_i = jnp.searchsorted(be_c, be_c, side="right")
    nxt = jnp.where(nxt_i < nb, be_c[jnp.minimum(nxt_i, nb - 1)], -1).astype(jnp.int32)

    def row(i, nr):
        return jnp.minimum(i, nr[0] - 1)

    return pl.pallas_call(
        functools.partial(_glu_kernel, part=min(MOE_PART, tm), half=min(MOE_SUB, tm), tn=tn, packed=packed),
        grid_spec=pltpu.PrefetchScalarGridSpec(
            num_scalar_prefetch=4, grid=(F // tn, nb),
            in_specs=[
                pl.BlockSpec((tm, x.shape[1]), lambda j, i, be, bv, nr, nx: (row(i, nr), 0)),
                pl.BlockSpec(memory_space=pl.ANY),
                pl.BlockSpec(memory_space=pl.ANY),
            ],
            out_specs=pl.BlockSpec((tm, tn), lambda j, i, be, bv, nr, nx: (i, j)),
            scratch_shapes=[pltpu.VMEM((2, D, tn), F32), pltpu.VMEM((2, D, tn), BF16),
                            pltpu.SemaphoreType.DMA((2,))]),
        out_shape=jax.ShapeDtypeStruct((P, F), BF16),
        compiler_params=_cparams("arbitrary", "arbitrary"),
        name="glu_up",
    )(block_e, block_valid, n_real, nxt, x, wg, wu)


def _down_kernel(be_ref, bv_ref, nr_ref, h_ref, w_ref, o_ref, *, part):
    del be_ref, nr_ref
    valid = bv_ref[pl.program_id(1)]

    @pl.when(pl.program_id(2) == 0)
    def _():
        o_ref[...] = jnp.zeros_like(o_ref)

    for q, pred in _row_variants(valid, h_ref.shape[0], part):
        @pl.when(pred)
        def _(q=q):
            o_ref[:q, :] += jnp.dot(h_ref[:q, :], w_ref[...].astype(BF16), preferred_element_type=F32)


def glu_down(h, wd, block_e, block_valid, n_real, *, tm, tn, tk):
    P, F = h.shape
    D = wd.shape[2]
    nb = P // tm

    def row(i, nr):
        return jnp.minimum(i, nr[0] - 1)

    return pl.pallas_call(
        functools.partial(_down_kernel, part=min(MOE_PART, tm)),
        grid_spec=pltpu.PrefetchScalarGridSpec(
            num_scalar_prefetch=3, grid=(D // tn, nb, F // tk),
            in_specs=[
                pl.BlockSpec((tm, tk), lambda n, i, k, be, bv, nr: (row(i, nr), k)),
                pl.BlockSpec((None, tk, tn), lambda n, i, k, be, bv, nr: (be[row(i, nr)], k, n)),
            ],
            out_specs=pl.BlockSpec((tm, tn), lambda n, i, k, be, bv, nr: (i, n))),
        out_shape=jax.ShapeDtypeStruct((P, D), F32),
        compiler_params=_cparams("arbitrary", "arbitrary", "arbitrary"),
        name="glu_down",
    )(block_e, block_valid, n_real, h, wd)


def _rot_half_cols(w):
    half = QK_ROPE_DIM // 2
    return jnp.concatenate([-w[..., half:], w[..., :half]], axis=-1)


def _rope_table(S):
    inv = 1.0 / (ROPE_THETA ** (jnp.arange(0, QK_ROPE_DIM, 2, dtype=F32) / QK_ROPE_DIM))
    ang = jnp.arange(S, dtype=F32)[:, None] * inv[None, :]
    c, s = jnp.cos(ang), jnp.sin(ang)
    return jnp.concatenate([c, c, s, s], axis=1)


def _tile_maps(seqs, tm):
    seq_of_tile, tile_pos = [], []
    for sid, (row0, length) in enumerate(seqs):
        assert row0 % tm == 0 and length % tm == 0
        for t in range(length // tm):
            seq_of_tile.append(sid)
            tile_pos.append(t)
    return jnp.asarray(np.array(seq_of_tile, np.int32)), jnp.asarray(np.array(tile_pos, np.int32))


def _moe_plan(logits, n_experts, rows):
    T = logits.shape[0]
    A = T * TOP_K
    top_vals, top_idx = lax.top_k(logits, TOP_K)
    probs = jax.nn.softmax(top_vals, axis=-1)
    flat_e = top_idx.reshape(A).astype(jnp.int32)
    flat_tok = jnp.repeat(jnp.arange(T, dtype=jnp.int32), TOP_K)
    order = jnp.argsort(flat_e)
    sorted_e = flat_e[order]
    counts = jnp.bincount(flat_e, length=n_experts).astype(jnp.int32)
    padded = (counts + rows - 1) // rows * rows
    padded_end = jnp.cumsum(padded)
    start = jnp.cumsum(counts) - counts
    start_pad = padded_end - padded
    dest = start_pad[sorted_e] + jnp.arange(A, dtype=jnp.int32) - start[sorted_e]
    n_blocks = (A + n_experts * (rows - 1) + rows - 1) // rows
    P = n_blocks * rows
    pos = dest[jnp.argsort(order)].reshape(T, TOP_K)
    block_start = jnp.arange(n_blocks, dtype=jnp.int32) * rows
    block_e = jnp.minimum(jnp.searchsorted(padded_end, block_start, side="right"), n_experts - 1).astype(jnp.int32)
    slot = jnp.arange(P, dtype=jnp.int32)
    slot_e = jnp.repeat(block_e, rows)
    k_in = slot - start_pad[slot_e]
    src = jnp.clip(start[slot_e] + k_in, 0, A - 1)
    slot_tok = jnp.where(k_in < counts[slot_e], flat_tok[order][src], 0)
    group_end = start_pad + counts
    block_valid = jnp.clip(group_end[block_e] - block_start, 0, rows)
    block_valid = jnp.where(block_start < padded_end[-1], block_valid, 0).astype(jnp.int32)
    n_real = (padded_end[-1] // rows).astype(jnp.int32)
    return slot_tok, pos, probs, block_e, block_valid, n_real


def kernel(x_prompt, x_sample, c_prompt, c_sample, w_ada, b_ada, norm_mix_pre, norm_mix_post, norm_ffn_pre, norm_ffn_post, w_in, q_norm, w_q_up, kv_norm, w_kv_up, conv_w, conv_b, w_rg_a, b_rg_a, w_rg_i, b_rg_i, rg_lambda, attn_out_norm, rnn_out_norm, w_out, w_ff_gate, w_ff_up, w_ff_down, w_router, w_exp_gate, w_exp_up, w_exp_down):
    B, S, D = x_prompt.shape
    DB, DS, _ = x_sample.shape
    depth = w_in.shape[0]
    T0, T1 = B * S, DB * DS
    T = T0 + T1
    q_rank = q_norm.shape[1]
    kv_rank = kv_norm.shape[1]
    H = w_q_up.shape[2] // (QK_NOPE_DIM + QK_ROPE_DIM)
    C = conv_w.shape[2]
    n_rnn_blocks = w_rg_a.shape[2]
    assert w_rg_a.shape[3] == LANES and C == n_rnn_blocks * LANES
    F_ff = w_ff_gate.shape[2]
    E = w_router.shape[2]
    scale = (QK_NOPE_DIM + QK_ROPE_DIM) ** -0.5

    seqs = [(b * S, S) for b in range(B)] + [(T0 + b * DS, DS) for b in range(DB)]
    s_min = min(S, DS)
    tm = min(1024, s_min)
    te = min(256, s_min)
    _, pos_tm = _tile_maps(seqs, tm)
    seq_te, _ = _tile_maps(seqs, te)
    cs_table = _rope_table(max(S, DS))

    xs = [x_prompt.reshape(T0, D), x_sample.reshape(T1, D)]
    n_seq = B + DB
    c_all = jnp.concatenate([c_prompt, c_sample, jnp.zeros((-n_seq % 16, D), F32)], axis=0)

    def vec(p):
        return p.reshape(1, -1)

    tn_ada = min(1024, 6 * D)
    mod = fk_matmul("ada", [c_all], [w_ada], w_lead_tiles=6 * D // tn_ada, prologue="silu",
                    bias=b_ada.reshape(1, depth * 6 * D), tm=c_all.shape[0], tn=tn_ada, out_dtype=F32)
    mod = mod.reshape(c_all.shape[0], depth * 6, D).transpose(1, 0, 2)

    tq = min(1024, s_min)
    rows = min(MOE_ROWS, s_min)
    tn_ff = min(512, F_ff)
    tk_ff = min(1024, F_ff)
    tn_dn = min(2048, D)

    hmix = prenorm(xs, vec(norm_mix_pre[0]), mod, seq_te, tm=te, sc=1, sh=0)

    for l in range(depth):
        m0 = 6 * l
        wl = w_in[l]
        n_lat = q_rank + kv_rank
        w_kr = wl[:, n_lat:n_lat + QK_ROPE_DIM]
        tn_qkv = next(t for t in (512, 256, 128) if n_lat % t == 0)
        assert q_rank % kv_rank == 0
        w_r = wl[:, n_lat + QK_ROPE_DIM:]
        qkv = fk_matmul("qkv_proj", [hmix], [wl[:, :n_lat]], tm=tm, tn=tn_qkv, out_dtype=F32)
        k_rope = fk_matmul("k_rope", [hmix], [jnp.concatenate([w_kr, _rot_half_cols(w_kr)], axis=1)],
                           tm=tm, tn=LANES, out_dtype=BF16, epilogue="rope", cs=cs_table, tile_pos=pos_tm)
        r = fk_matmul("r_proj", [hmix], [w_r], tm=tm, tn=min(512, 2 * C), out_dtype=F32)

        wq = w_q_up[l].reshape(q_rank, H, QK_NOPE_DIM + QK_ROPE_DIM)
        w_qn = wq[:, :, :QK_NOPE_DIM].reshape(q_rank, H * QK_NOPE_DIM)
        wq_r = wq[:, :, QK_NOPE_DIM:]
        w_qr = jnp.concatenate([wq_r, _rot_half_cols(wq_r)], axis=-1).reshape(q_rank, H * LANES)
        wkv = w_kv_up[l].reshape(kv_rank, H, QK_NOPE_DIM + V_HEAD_DIM)
        w_kv = jnp.concatenate([wkv[:, :, :QK_NOPE_DIM].reshape(kv_rank, H * QK_NOPE_DIM),
                                wkv[:, :, QK_NOPE_DIM:].reshape(kv_rank, H * V_HEAD_DIM)], axis=1)
        qg = vec(q_norm[l])
        q_cols = [(q_rank, 0)]
        kv_cols = [(kv_rank, q_rank // kv_rank)]
        tn_q = min(1024, w_qn.shape[1])
        q_up = fk_matmul("q_up", [qkv], [jnp.concatenate([w_qn, w_qr], axis=1)], x_cols=q_cols, gs=[qg],
                         prologue="rms", tm=tm, tn=tn_q, out_dtype=BF16, epilogue="rope",
                         rope_from=w_qn.shape[1] // tn_q, cs=cs_table, tile_pos=pos_tm)
        kv = fk_matmul("kv_up", [qkv], [w_kv], x_cols=kv_cols, gs=[vec(kv_norm[l])], prologue="rms", tm=tm,
                       tn=min(1024, w_kv.shape[1]), out_dtype=BF16)

        y_att = jnp.zeros((T, H * V_HEAD_DIM), BF16)
        y_att = attention(q_up, kv, k_rope, y_att, n_heads=H, n_rows=T, batch=B, seq=S, row0=0,
                          tq=min(tq, S), scale=scale, name="attn_prompt")
        y_att = attention(q_up, kv, k_rope, y_att, n_heads=H, n_rows=T, batch=DB, seq=DS, row0=T0,
                          tq=min(tq, DS), scale=scale, name="attn_sample")

        rp = jnp.concatenate([conv_w[l], conv_b[l][None], 0.5 * b_rg_a[l], 0.5 * b_rg_i[l], rg_lambda[l],
                              jnp.zeros((5, C), F32)], axis=0)
        w_cat = 0.5 * jnp.concatenate([w_rg_a[l, 0], w_rg_a[l, 1], w_rg_i[l, 0], w_rg_i[l, 1]], axis=-1)
        tc = min(256, C)
        y_rnn = jnp.zeros((T, C), BF16)
        y_rnn = rglru(r, rp, w_cat, y_rnn, n_rows=T, batch=B, seq=S, row0=0, tc=tc, name="rglru_prompt")
        y_rnn = rglru(r, rp, w_cat, y_rnn, n_rows=T, batch=DB, seq=DS, row0=T0, tc=tc, name="rglru_sample")

        assert w_out.shape[1] == 2 * y_att.shape[1] and y_rnn.shape[1] == y_att.shape[1]
        y_mix = fk_matmul("out_proj", [y_att, y_rnn], [w_out, w_out], w_lead=l, w_rows=(0, 1),
                          gs=[vec(attn_out_norm[l]), vec(rnn_out_norm[l])], prologue="rms",
                          tm=tm, tn=min(512, D), out_dtype=F32)

        j = l // 2
        moe = None
        if l % 2 == 0:
            x, hff = post_residual(xs, y_mix, vec(norm_mix_post[l]), mod, seq_te, tm=te, gate=m0 + 2,
                                   gpre=vec(norm_ffn_pre[l]), sc=m0 + 4, sh=m0 + 3)
            nb = T // rows
            be = jnp.full((nb,), j, jnp.int32)
            bv = jnp.full((nb,), rows, jnp.int32)
            nr = jnp.full((1,), nb, jnp.int32)
            hidden = glu_up(hff, w_ff_gate, w_ff_up, be, bv, nr, tm=rows, tn=tn_ff)
            f = glu_down(hidden, w_ff_down, be, bv, nr, tm=rows, tn=tn_dn, tk=tk_ff)
        else:
            w_r_pad = jnp.concatenate([w_router[j], jnp.zeros((D, LANES - E), F32)], axis=1)
            x, hff, logits = post_residual(xs, y_mix, vec(norm_mix_post[l]), mod, seq_te, tm=te, gate=m0 + 2,
                                           gpre=vec(norm_ffn_pre[l]), sc=m0 + 4, sh=m0 + 3, w_router=w_r_pad,
                                           pack=True)
            slot_tok, pos, probs, be, bv, nr = _moe_plan(logits[:, :E], E, rows)
            be = be + j * E
            nr = nr.reshape(1)
            x_sorted = gather_rows(hff, slot_tok, rows=rows)
            n_moe = w_exp_gate.shape[0]
            hidden = glu_up(x_sorted, w_exp_gate.reshape(n_moe * E, D, F_ff),
                            w_exp_up.reshape(n_moe * E, D, F_ff), be, bv, nr, tm=rows, tn=tn_ff, packed=True)
            ys = glu_down(hidden, w_exp_down.reshape(n_moe * E, F_ff, D), be, bv, nr,
                          tm=rows, tn=tn_dn, tk=tk_ff)
            f, moe = None, (ys, pos, probs)
        xs = [x]

        if l + 1 < depth:
            x, hmix = post_residual(xs, f, vec(norm_ffn_post[l]), mod, seq_te, tm=te, gate=m0 + 5, moe=moe,
                                    gpre=vec(norm_mix_pre[l + 1]), sc=m0 + 7, sh=m0 + 6)
            xs = [x]
        else:
            y_prompt, y_sample = post_residual(xs, f, vec(norm_ffn_post[l]), mod, seq_te, tm=te, gate=m0 + 5,
                                               moe=moe, out_rows=(T0, T1))

    return (y_prompt.reshape(B, S, D), y_sample.reshape(DB, DS, D))
```

```python
import functools

import numpy as np
import jax
import jax.numpy as jnp
from jax import lax
from jax.experimental import pallas as pl
from jax.experimental.pallas import tpu as pltpu

F32 = jnp.float32
BF16 = jnp.bfloat16

EPS = 1e-6
QK_NOPE_DIM = 128
QK_ROPE_DIM = 64
V_HEAD_DIM = 128
ROPE_THETA = 10000.0
RG_C = 8.0
LOG2_E = 1.4426950408889634
TOP_K = 2
LANES = 128
VMEM_LIMIT = 56 * 1024 * 1024
MOE_ROWS = 1024
MOE_SUB = 512
MOE_PART = 256


def _cparams(*sem):
    return pltpu.CompilerParams(dimension_semantics=sem, vmem_limit_bytes=VMEM_LIMIT)


def _sigmoid(x):
    return 1.0 / (1.0 + jnp.exp(-x))


def _rms(x, g):
    return x * lax.rsqrt(jnp.mean(x * x, axis=-1, keepdims=True) + EPS) * g


def _rope_groups(acc, cs):
    m, n = acc.shape
    reps = n // LANES
    t = acc * (jnp.tile(cs, (1, reps)) if reps > 1 else cs)
    low = lax.broadcasted_iota(jnp.int32, (m, n), 1) % LANES < QK_ROPE_DIM
    outs = []
    for r in range(reps):
        tr = t[:, r * LANES:(r + 1) * LANES]
        outs.append(tr + pltpu.roll(tr, QK_ROPE_DIM, 1))
    y = outs[0] if reps == 1 else jnp.concatenate(outs, axis=1)
    return jnp.where(low, y, 0.0)


def _fk_kernel(pos_ref, *refs, n_in, prologue, has_bias, epilogue, rope_from):
    del pos_ref
    refs = list(refs)
    xs = [refs.pop(0) for _ in range(n_in)]
    gs = [refs.pop(0) for _ in range(n_in)] if prologue == "rms" else []
    ws = [refs.pop(0) for _ in range(n_in)]
    bias = refs.pop(0) if has_bias else None
    cs = refs.pop(0) if epilogue is not None else None
    out = refs.pop(0)
    xbs = refs

    if prologue is not None:
        @pl.when(pl.program_id(1) == 0)
        def _():
            for i in range(n_in):
                x = xs[i][...].astype(F32)
                if prologue == "rms":
                    x = _rms(x, gs[i][...])
                else:
                    x = x * _sigmoid(x)
                xbs[i][...] = x.astype(BF16)
        lhs = [xb[...] for xb in xbs]
    else:
        lhs = [x[...] for x in xs]

    acc = None
    for i in range(n_in):
        part = jnp.dot(lhs[i], ws[i][...].astype(BF16), preferred_element_type=F32)
        acc = part if acc is None else acc + part
    if has_bias:
        acc = acc + bias[...]
    if epilogue == "rope":
        roped = pl.program_id(1) >= rope_from

        @pl.when(roped)
        def _():
            out[...] = _rope_groups(acc, cs[...]).astype(out.dtype)

        @pl.when(jnp.logical_not(roped))
        def _():
            out[...] = acc.astype(out.dtype)
    else:
        out[...] = acc.astype(out.dtype)


def fk_matmul(name, xs, ws, *, tm, tn, out_dtype, x_cols=None, gs=None, prologue=None, bias=None,
              w_lead=None, w_rows=None, w_lead_tiles=None, n_cols=None, epilogue=None, rope_from=0, cs=None,
              tile_pos=None):
    n_in = len(xs)
    M = xs[0].shape[0]
    N = ws[0].shape[-1] if n_cols is None else n_cols
    if w_lead_tiles is not None:
        assert w_lead_tiles * tn == ws[0].shape[-1]
        N = ws[0].shape[0] * ws[0].shape[-1]
    assert M % tm == 0 and N % tn == 0
    grid = (M // tm, N // tn)
    if tile_pos is None:
        tile_pos = jnp.zeros((grid[0],), jnp.int32)
    if x_cols is None:
        x_cols = [(x.shape[1], 0) for x in xs]

    in_specs, args = [], []
    for x, (kw, cb) in zip(xs, x_cols):
        in_specs.append(pl.BlockSpec((tm, kw), lambda i, j, p, cb=cb: (i, cb)))
        args.append(x)
    if prologue == "rms":
        for g in gs:
            in_specs.append(pl.BlockSpec((1, g.shape[1]), lambda i, j, p: (0, 0)))
            args.append(g)
    for idx, ((kw, _), w) in enumerate(zip(x_cols, ws)):
        if w.ndim == 3:
            rb = 0 if w_rows is None else w_rows[idx]
            if w_lead_tiles is None:
                in_specs.append(pl.BlockSpec((None, kw, tn), lambda i, j, p, rb=rb: (w_lead, rb, j)))
            else:
                nt = w_lead_tiles
                in_specs.append(pl.BlockSpec((None, kw, tn), lambda i, j, p, rb=rb: (j // nt, rb, j % nt)))
        else:
            assert w.shape[0] == kw
            in_specs.append(pl.BlockSpec((kw, tn), lambda i, j, p: (0, j)))
        args.append(w)
    if bias is not None:
        in_specs.append(pl.BlockSpec((1, tn), lambda i, j, p: (0, j)))
        args.append(bias)
    if epilogue is not None:
        in_specs.append(pl.BlockSpec((tm, LANES), lambda i, j, p: (p[i], 0)))
        args.append(cs)

    scratch = []
    if prologue is not None:
        scratch = [pltpu.VMEM((tm, kw), BF16) for kw, _ in x_cols]

    kern = functools.partial(_fk_kernel, n_in=n_in, prologue=prologue, has_bias=bias is not None,
                             epilogue=epilogue, rope_from=rope_from)
    return pl.pallas_call(
        kern,
        grid_spec=pltpu.PrefetchScalarGridSpec(
            num_scalar_prefetch=1, grid=grid, in_specs=in_specs,
            out_specs=pl.BlockSpec((tm, tn), lambda i, j, p: (i, j)),
            scratch_shapes=scratch),
        out_shape=jax.ShapeDtypeStruct((M, N), out_dtype),
        compiler_params=_cparams("parallel", "arbitrary"),
        name=name,
    )(tile_pos, *args)


def _mod_row(mod_ref, chunk, s):
    return mod_ref[chunk, pl.ds(s, 1), :]


def _split_rows_specs(parts, tm, width):
    specs, start = [], 0
    for p in parts:
        n = p.shape[0] // tm
        specs.append(pl.BlockSpec((tm, width), lambda i, *_, start=start, n=n: (jnp.clip(i - start, 0, n - 1), 0)))
        start += n
    return specs


def _prenorm_kernel(seq_ref, *refs, sc, sh, n_parts, part_tiles):
    xs = refs[:n_parts]
    g_ref, mod_ref, h_ref = refs[n_parts:]
    i = pl.program_id(0)
    s = seq_ref[i]
    x = _select_part(xs, part_tiles, i)
    y = _rms(x, g_ref[...])
    h_ref[...] = (y * (1.0 + _mod_row(mod_ref, sc, s)) + _mod_row(mod_ref, sh, s)).astype(h_ref.dtype)


def _select_part(refs, part_tiles, i):
    x = refs[-1][...]
    end = sum(part_tiles[:-1])
    for k in range(len(refs) - 2, -1, -1):
        x = jnp.where(i < end, refs[k][...], x)
        end -= part_tiles[k]
    return x


def prenorm(xs, g, mod, seq_of_tile, *, tm, sc, sh):
    D = xs[0].shape[1]
    T = sum(x.shape[0] for x in xs)
    part_tiles = tuple(x.shape[0] // tm for x in xs)
    return pl.pallas_call(
        functools.partial(_prenorm_kernel, sc=sc, sh=sh, n_parts=len(xs), part_tiles=part_tiles),
        grid_spec=pltpu.PrefetchScalarGridSpec(
            num_scalar_prefetch=1, grid=(T // tm,),
            in_specs=_split_rows_specs(xs, tm, D) + [
                pl.BlockSpec((1, D), lambda i, s: (0, 0)),
                pl.BlockSpec(mod.shape, lambda i, s: (0, 0, 0))],
            out_specs=pl.BlockSpec((tm, D), lambda i, s: (i, 0))),
        out_shape=jax.ShapeDtypeStruct((T, D), BF16),
        compiler_params=_cparams("parallel"),
        name="prenorm",
    )(seq_of_tile, *xs, g, mod)


def _pack_bf16_pairs(h):
    n = h.shape[1] // 2
    lo = lax.bitcast_convert_type(h[:, :n].astype(BF16).astype(F32), jnp.uint32)
    hi = lax.bitcast_convert_type(h[:, n:].astype(BF16).astype(F32), jnp.uint32)
    return (lo >> 16) | (hi & jnp.uint32(0xFFFF0000))


def _unpack_bf16_pairs(w):
    lo = lax.bitcast_convert_type(w << 16, F32).astype(BF16)
    hi = lax.bitcast_convert_type(w & jnp.uint32(0xFFFF0000), F32).astype(BF16)
    return lo, hi


def _row_dma(src_hbm, row, dst, sem):
    return pltpu.make_async_copy(src_hbm.at[pl.ds(row, 1), :], dst, sem)


def _post_kernel(seq_ref, *refs, gate, sc, sh, n_x, x_tiles, moe, has_next, pack, has_router, out_tiles):
    refs = list(refs)
    xs = [refs.pop(0) for _ in range(n_x)]
    if moe:
        probs_ref, pos_ref, pos_next_ref, ys_hbm = [refs.pop(0) for _ in range(4)]
    else:
        y_ref = refs.pop(0)
    gpost_ref, mod_ref = refs.pop(0), refs.pop(0)
    gpre_ref = refs.pop(0) if has_next else None
    wr_ref = refs.pop(0) if has_router else None
    xo_refs = [refs.pop(0) for _ in out_tiles]
    h_ref = refs.pop(0) if has_next else None
    lg_ref = refs.pop(0) if has_router else None
    i = pl.program_id(0)
    s = seq_ref[i]
    tm = xs[0].shape[0]

    if moe:
        buf, sem = refs
        slot = lax.rem(i, 2)

        def fetch(pos, slot_):
            def body(t, c):
                for k in range(TOP_K):
                    _row_dma(ys_hbm, pos[0, 0, TOP_K * t + k], buf.at[slot_, k, pl.ds(t, 1), :],
                             sem.at[slot_]).start()
                return c
            lax.fori_loop(0, tm, body, 0, unroll=8)

        @pl.when(i == 0)
        def _():
            fetch(pos_ref, 0)

        @pl.when(i + 1 < pl.num_programs(0))
        def _():
            fetch(pos_next_ref, 1 - slot)

        def wait(t, c):
            for k in range(TOP_K):
                _row_dma(ys_hbm, 0, buf.at[slot, k, pl.ds(t, 1), :], sem.at[slot]).wait()
            return c
        lax.fori_loop(0, tm, wait, 0, unroll=8)
        y = probs_ref[:, 0:1] * buf[slot, 0]
        for k in range(1, TOP_K):
            y = y + probs_ref[:, k:k + 1] * buf[slot, k]
    else:
        y = y_ref[...]

    xn = _select_part(xs, x_tiles, i) + _mod_row(mod_ref, gate, s) * _rms(y, gpost_ref[...])
    start = 0
    for xo_ref, n in zip(xo_refs, out_tiles):
        if len(xo_refs) == 1:
            xo_ref[...] = xn
        else:
            @pl.when(jnp.logical_and(i >= start, i < start + n))
            def _(xo_ref=xo_ref):
                xo_ref[...] = xn
        start += n
    if has_next:
        h = _rms(xn, gpre_ref[...]) * (1.0 + _mod_row(mod_ref, sc, s)) + _mod_row(mod_ref, sh, s)
        h_ref[...] = _pack_bf16_pairs(h) if pack else h.astype(h_ref.dtype)
        if has_router:
            lg_ref[...] = jnp.dot(h, wr_ref[...], preferred_element_type=F32,
                                  precision=lax.Precision.HIGHEST)


def post_residual(xs, y, gpost, mod, seq_of_tile, *, tm, gate, gpre=None, sc=None, sh=None, w_router=None,
                  pack=False, moe=None, out_rows=None):
    D = xs[0].shape[1]
    T = sum(x.shape[0] for x in xs)
    nt = T // tm
    has_next = gpre is not None
    has_router = w_router is not None
    row = pl.BlockSpec((tm, D), lambda i, s: (i, 0))
    vec = pl.BlockSpec((1, D), lambda i, s: (0, 0))
    in_specs = _split_rows_specs(xs, tm, D)
    args = list(xs)
    scratch = []
    if moe is not None:
        ys, pos, probs = moe
        pos3 = pos.reshape(nt, 1, TOP_K * tm)
        in_specs += [
            pl.BlockSpec((tm, TOP_K), lambda i, s: (i, 0)),
            pl.BlockSpec((1, 1, TOP_K * tm), lambda i, s: (i, 0, 0), memory_space=pltpu.SMEM),
            pl.BlockSpec((1, 1, TOP_K * tm), lambda i, s: (jnp.minimum(i + 1, nt - 1), 0, 0),
                         memory_space=pltpu.SMEM),
            pl.BlockSpec(memory_space=pl.ANY),
        ]
        args += [probs, pos3, pos3, ys]
        scratch = [pltpu.VMEM((2, TOP_K, tm, D), F32), pltpu.SemaphoreType.DMA((2,))]
    else:
        in_specs.append(row)
        args.append(y)
    in_specs += [vec, pl.BlockSpec(mod.shape, lambda i, s: (0, 0, 0))]
    args += [gpost, mod]
    if has_next:
        in_specs.append(vec)
        args.append(gpre)
    if has_router:
        in_specs.append(pl.BlockSpec(w_router.shape, lambda i, s: (0, 0)))
        args.append(w_router)

    if out_rows is None:
        out_rows = (T,)
    out_tiles = tuple(r // tm for r in out_rows)
    outs = [jax.ShapeDtypeStruct((r, D), F32) for r in out_rows]
    out_shape = list(outs)
    out_specs = _split_rows_specs(outs, tm, D) if len(outs) > 1 else [row]
    if has_next:
        if pack:
            out_shape.append(jax.ShapeDtypeStruct((T, D // 2), jnp.uint32))
            out_specs.append(pl.BlockSpec((tm, D // 2), lambda i, s: (i, 0)))
        else:
            out_shape.append(jax.ShapeDtypeStruct((T, D), BF16))
            out_specs.append(row)
    if has_router:
        out_shape.append(jax.ShapeDtypeStruct((T, w_router.shape[1]), F32))
        out_specs.append(pl.BlockSpec((tm, w_router.shape[1]), lambda i, s: (i, 0)))
    kern = functools.partial(
        _post_kernel, gate=gate, sc=sc, sh=sh, n_x=len(xs), x_tiles=tuple(x.shape[0] // tm for x in xs),
        moe=moe is not None, has_next=has_next, pack=pack, has_router=has_router, out_tiles=out_tiles)
    return pl.pallas_call(
        kern,
        grid_spec=pltpu.PrefetchScalarGridSpec(
            num_scalar_prefetch=1, grid=(nt,), in_specs=in_specs, out_specs=out_specs,
            scratch_shapes=scratch),
        out_shape=out_shape,
        compiler_params=_cparams("arbitrary"),
        name="post_residual",
    )(seq_of_tile, *args)


def _attn_kernel(qn_ref, qr_ref, kn_ref, kr_ref, v_ref, *rest, scale, ck):
    o_ref = rest[-1]
    tq = qn_ref.shape[0]
    S = kn_ref.shape[0]
    q = jnp.concatenate([qn_ref[...], qr_ref[...]], axis=1)
    c2 = scale * LOG2_E
    m = jnp.full((tq, 1), -jnp.inf, F32)
    lpart = jnp.zeros((tq, LANES), F32)
    acc = jnp.zeros((tq, V_HEAD_DIM), F32)
    for c in range(S // ck):
        rows = slice(c * ck, (c + 1) * ck)
        k = jnp.concatenate([kn_ref[rows, :], kr_ref[rows, :]], axis=1)
        s = lax.dot_general(q, k, (((1,), (1,)), ((), ())), preferred_element_type=F32)
        mpart = s[:, :LANES]
        for g in range(1, ck // LANES):
            mpart = jnp.maximum(mpart, s[:, g * LANES:(g + 1) * LANES])
        m_new = jnp.maximum(m, jnp.max(mpart, axis=1, keepdims=True))
        alpha = jnp.exp2((m - m_new) * c2)
        p = jnp.exp2(s * c2 - m_new * c2)
        psum = p[:, :LANES]
        for g in range(1, ck // LANES):
            psum = psum + p[:, g * LANES:(g + 1) * LANES]
        lpart = alpha * lpart + psum
        acc = alpha * acc + jnp.dot(p.astype(BF16), v_ref[rows, :], preferred_element_type=F32)
        m = m_new
    l = jnp.sum(lpart, axis=1, keepdims=True)
    o_ref[...] = (acc / l).astype(o_ref.dtype)


def attention(q, kv, kr, prev, *, n_heads, n_rows, batch, seq, row0, tq, scale, name):
    assert row0 % seq == 0 and seq % tq == 0
    b0 = row0 // seq
    nq = seq // tq
    q0 = row0 // tq
    H = n_heads
    ck = min(1024, seq)
    in_specs = [
        pl.BlockSpec((tq, LANES), lambda b, h, i: (q0 + b * nq + i, h)),
        pl.BlockSpec((tq, LANES), lambda b, h, i: (q0 + b * nq + i, H + h)),
        pl.BlockSpec((seq, LANES), lambda b, h, i: (b0 + b, h)),
        pl.BlockSpec((seq, LANES), lambda b, h, i: (b0 + b, 0)),
        pl.BlockSpec((seq, LANES), lambda b, h, i: (b0 + b, H + h)),
    ]
    args = [q, q, kv, kr, kv]
    aliases = {}
    if prev is not None:
        in_specs.append(pl.BlockSpec(memory_space=pl.ANY))
        args.append(prev)
        aliases = {5: 0}
    return pl.pallas_call(
        functools.partial(_attn_kernel, scale=scale, ck=ck),
        grid=(batch, H, nq),
        in_specs=in_specs,
        out_specs=pl.BlockSpec((tq, LANES), lambda b, h, i: (q0 + b * nq + i, h)),
        out_shape=jax.ShapeDtypeStruct((n_rows, H * V_HEAD_DIM), BF16),
        input_output_aliases=aliases,
        compiler_params=_cparams("parallel", "parallel", "arbitrary"),
        name=name,
    )(*args)


def _tile_scan(a_ref, b_ref, n, row0, n_tiles, reverse):
    js = list(range(8))[::-1] if reverse else list(range(8))
    slab = lambda j: (n, pl.ds(row0 + j, n_tiles, stride=8), slice(None))
    A = a_ref[slab(js[0])]
    B = b_ref[slab(js[0])]
    for j in js[1:]:
        aj = a_ref[slab(j)]
        B = aj * B + b_ref[slab(j)]
        A = aj * A
        a_ref[slab(j)] = A
        b_ref[slab(j)] = B


def _rglru_kernel(rx_ref, rg_ref, rp_ref, w_ref, *rest, tt):
    o_ref, xpad, a0, b0, a1, b1 = rest[-6:]
    S, tc = rx_ref.shape
    nb = tc // LANES
    zeros8 = jnp.zeros((8, tc), F32)
    xpad[0:8, :] = zeros8
    xpad[8 + S:16 + S, :] = zeros8
    xpad[8:8 + S, :] = rx_ref[...]
    rp = rp_ref[...]
    cw = [rp[j:j + 1, :] for j in range(4)]
    cb = rp[4:5, :]
    hb_a = (rp[5:6, :], rp[6:7, :])
    hb_i = (rp[7:8, :], rp[8:9, :])
    e2 = tuple((-0.5 * RG_C * LOG2_E) * (jnp.maximum(-lam, 0.0) + jnp.log1p(jnp.exp(-jnp.abs(lam))))
               for lam in (rp[9:10, :], rp[10:11, :]))
    ab = ((a0, b0), (a1, b1))

    def chunk(c, carry):
        c0 = pl.multiple_of(c * tt, tt)
        xw = xpad[pl.ds(c0, tt + 16), :]
        xc = (cw[0] * xw[6:6 + tt] + cw[1] * xw[7:7 + tt] + cw[2] * xw[8:8 + tt] + cw[3] * xw[9:9 + tt]) + cb
        for n in range(nb):
            sl = slice(n * LANES, (n + 1) * LANES)
            xcn = xc[:, sl]
            z = jnp.dot(xcn.astype(BF16), w_ref[n].astype(BF16), preferred_element_type=F32)
            xh = 0.5 * xcn
            for d in range(2):
                th_a = jnp.tanh(z[:, d * LANES:(d + 1) * LANES] + hb_a[d][:, sl])
                th_i = jnp.tanh(z[:, (2 + d) * LANES:(3 + d) * LANES] + hb_i[d][:, sl])
                a = jnp.exp2(e2[d][:, sl] * (1.0 + th_a))
                u = 1.0 - a * a
                gain = u * lax.rsqrt(jnp.maximum(u, 1e-30))
                ab[d][0][n, pl.ds(c0, tt), :] = a
                ab[d][1][n, pl.ds(c0, tt), :] = gain * (1.0 + th_i) * xh
        for n in range(nb):
            for d in range(2):
                _tile_scan(ab[d][0], ab[d][1], n, c0, tt // 8, reverse=(d == 1))
        return carry

    lax.fori_loop(0, S // tt, chunk, 0)

    def scan(t, hs):
        rf = pl.multiple_of(t * 8, 8)
        rb = pl.multiple_of(S - 8 - t * 8, 8)
        out = []
        for n in range(nb):
            hf, hb = hs[2 * n], hs[2 * n + 1]
            h = a0[n, pl.ds(rf, 8), :] * hf + b0[n, pl.ds(rf, 8), :]
            b0[n, pl.ds(rf, 8), :] = h
            out.append(jnp.broadcast_to(h[7:8, :], (8, LANES)))
            h = a1[n, pl.ds(rb, 8), :] * hb + b1[n, pl.ds(rb, 8), :]
            b1[n, pl.ds(rb, 8), :] = h
            out.append(jnp.broadcast_to(h[0:1, :], (8, LANES)))
        return tuple(out)

    h0 = jnp.zeros((8, LANES), F32)
    lax.fori_loop(0, S // 8, scan, (h0,) * (2 * nb), unroll=4)

    def gate(c, carry):
        c0 = pl.multiple_of(c * tt, tt)
        g = rg_ref[pl.ds(c0, tt), :]
        gelu = 0.5 * g * (1.0 + jnp.tanh(0.7978845608028654 * (g + 0.044715 * (g * g * g))))
        hsum = jnp.concatenate([b0[n, pl.ds(c0, tt), :] + b1[n, pl.ds(c0, tt), :] for n in range(nb)], axis=1)
        o_ref[pl.ds(c0, tt), :] = (gelu * hsum).astype(o_ref.dtype)
        return carry

    lax.fori_loop(0, S // tt, gate, 0)


def rglru(r, rp, w_cat, prev, *, n_rows, batch, seq, row0, tc, name):
    C = r.shape[1] // 2
    assert row0 % seq == 0 and C % tc == 0
    b0 = row0 // seq
    nct = C // tc
    tt = min(256, seq)
    in_specs = [
        pl.BlockSpec((seq, tc), lambda b, c: (b0 + b, c)),
        pl.BlockSpec((seq, tc), lambda b, c: (b0 + b, nct + c)),
        pl.BlockSpec((16, tc), lambda b, c: (0, c)),
        pl.BlockSpec((tc // LANES, LANES, 4 * LANES), lambda b, c: (c, 0, 0)),
    ]
    args = [r, r, rp, w_cat]
    aliases = {}
    if prev is not None:
        in_specs.append(pl.BlockSpec(memory_space=pl.ANY))
        args.append(prev)
        aliases = {4: 0}
    return pl.pallas_call(
        functools.partial(_rglru_kernel, tt=tt),
        grid=(batch, nct),
        in_specs=in_specs,
        out_specs=pl.BlockSpec((seq, tc), lambda b, c: (b0 + b, c)),
        out_shape=jax.ShapeDtypeStruct((n_rows, C), BF16),
        scratch_shapes=[pltpu.VMEM((seq + 16, tc), F32)] + [pltpu.VMEM((tc // LANES, seq, LANES), F32)] * 4,
        input_output_aliases=aliases,
        compiler_params=_cparams("parallel", "parallel"),
        name=name,
    )(*args)


def _gather_rows_kernel(idx_ref, x_hbm, o_ref, sem, *, rows):
    def start(g, c):
        for j in range(8):
            r = g * 8 + j
            _row_dma(x_hbm, idx_ref[0, 0, r], o_ref.at[pl.ds(r, 1), :], sem).start(priority=j % 2)
        return c
    lax.fori_loop(0, rows // 8, start, 0)

    def wait(r, c):
        _row_dma(x_hbm, 0, o_ref.at[pl.ds(r, 1), :], sem).wait()
        return c
    lax.fori_loop(0, rows, wait, 0, unroll=8)


def gather_rows(x, idx, *, rows):
    P = idx.shape[0]
    assert P % rows == 0 and rows % 8 == 0
    return pl.pallas_call(
        functools.partial(_gather_rows_kernel, rows=rows),
        grid=(P // rows,),
        in_specs=[pl.BlockSpec((1, 1, rows), lambda i: (i, 0, 0), memory_space=pltpu.SMEM),
                  pl.BlockSpec(memory_space=pl.ANY)],
        out_specs=pl.BlockSpec((rows, x.shape[1]), lambda i: (i, 0)),
        out_shape=jax.ShapeDtypeStruct((P, x.shape[1]), x.dtype),
        scratch_shapes=[pltpu.SemaphoreType.DMA(())],
        compiler_params=_cparams("arbitrary"),
        name="gather_rows",
    )(idx.reshape(P // rows, 1, rows), x)


def _row_variants(valid, tm, part):
    return [(q, jnp.logical_and(valid > q - part, valid <= q)) for q in range(part, tm + 1, part)]


def _glu_kernel(be_ref, bv_ref, nr_ref, nxt_ref, x_ref, wg_hbm, wu_hbm, o_ref, land, wb, sem, *,
                part, half, tn, packed):
    j, i = pl.program_id(0), pl.program_id(1)
    last = nr_ref[0] - 1
    ie = jnp.minimum(i, last)
    e = be_ref[ie]
    changed = jnp.logical_or(i == 0, e != be_ref[jnp.clip(i - 1, 0, last)])

    def weight_copies(e_, j_):
        col = pl.multiple_of(j_ * tn, tn)
        return [pltpu.make_async_copy(w.at[e_, :, pl.ds(col, tn)], land.at[k], sem.at[k])
                for k, w in enumerate((wg_hbm, wu_hbm))]

    @pl.when(changed)
    def _():
        @pl.when(jnp.logical_and(j == 0, i == 0))
        def _():
            for c in weight_copies(e, j):
                c.start()
        for c in weight_copies(e, j):
            c.wait()
        for k in range(2):
            wb[k] = land[k].astype(BF16)
        nxt = nxt_ref[ie]

        @pl.when(nxt >= 0)
        def _():
            for c in weight_copies(nxt, j):
                c.start()

        @pl.when(jnp.logical_and(nxt < 0, j + 1 < pl.num_programs(0)))
        def _():
            for c in weight_copies(be_ref[0], j + 1):
                c.start()

    valid = bv_ref[i]
    for r0 in range(0, x_ref.shape[0], half):
        v_piece = jnp.clip(valid - r0, 0, half)

        @pl.when(v_piece == 0)
        def _(r0=r0):
            o_ref[r0:r0 + half, :] = jnp.zeros((half, o_ref.shape[1]), o_ref.dtype)

        for q, pred in _row_variants(v_piece, half, part):
            @pl.when(pred)
            def _(q=q, r0=r0):
                if packed:
                    kh = wb.shape[1] // 2
                    x_lo, x_hi = _unpack_bf16_pairs(x_ref[r0:r0 + q, :])
                    g, u = [jnp.dot(x_lo, wb[k, :kh, :], preferred_element_type=F32)
                            + jnp.dot(x_hi, wb[k, kh:, :], preferred_element_type=F32) for k in range(2)]
                else:
                    x = x_ref[r0:r0 + q, :]
                    g, u = [jnp.dot(x, wb[k], preferred_element_type=F32) for k in range(2)]
                o_ref[r0:r0 + q, :] = (g * _sigmoid(g) * u).astype(o_ref.dtype)
                if q < half:
                    o_ref[r0 + q:r0 + half, :] = jnp.zeros((half - q, o_ref.shape[1]), o_ref.dtype)


def glu_up(x, wg, wu, block_e, block_valid, n_real, *, tm, tn, packed=False):
    P = x.shape[0]
    _, D, F = wg.shape
    nb = P // tm
    be_c = block_e[jnp.minimum(jnp.arange(nb), n_real[0] - 1)]
    nxt_i = jnp.searchsorted(be_c, be_c, side="right")
    nxt = jnp.where(nxt_i < nb, be_c[jnp.minimum(nxt_i, nb - 1)], -1).astype(jnp.int32)

    def row(i, nr):
        return jnp.minimum(i, nr[0] - 1)

    return pl.pallas_call(
        functools.partial(_glu_kernel, part=min(MOE_PART, tm), half=min(MOE_SUB, tm), tn=tn, packed=packed),
        grid_spec=pltpu.PrefetchScalarGridSpec(
            num_scalar_prefetch=4, grid=(F // tn, nb),
            in_specs=[
                pl.BlockSpec((tm, x.shape[1]), lambda j, i, be, bv, nr, nx: (row(i, nr), 0)),
                pl.BlockSpec(memory_space=pl.ANY),
                pl.BlockSpec(memory_space=pl.ANY),
            ],
            out_specs=pl.BlockSpec((tm, tn), lambda j, i, be, bv, nr, nx: (i, j)),
            scratch_shapes=[pltpu.VMEM((2, D, tn), F32), pltpu.VMEM((2, D, tn), BF16),
                            pltpu.SemaphoreType.DMA((2,))]),
        out_shape=jax.ShapeDtypeStruct((P, F), BF16),
        compiler_params=_cparams("arbitrary", "arbitrary"),
        name="glu_up",
    )(block_e, block_valid, n_real, nxt, x, wg, wu)


def _down_kernel(be_ref, bv_ref, nr_ref, h_ref, w_ref, o_ref, *, part):
    del be_ref, nr_ref
    valid = bv_ref[pl.program_id(1)]

    @pl.when(pl.program_id(2) == 0)
    def _():
        o_ref[...] = jnp.zeros_like(o_ref)

    for q, pred in _row_variants(valid, h_ref.shape[0], part):
        @pl.when(pred)
        def _(q=q):
            o_ref[:q, :] += jnp.dot(h_ref[:q, :], w_ref[...].astype(BF16), preferred_element_type=F32)


def glu_down(h, wd, block_e, block_valid, n_real, *, tm, tn, tk):
    P, F = h.shape
    D = wd.shape[2]
    nb = P // tm

    def row(i, nr):
        return jnp.minimum(i, nr[0] - 1)

    return pl.pallas_call(
        functools.partial(_down_kernel, part=min(MOE_PART, tm)),
        grid_spec=pltpu.PrefetchScalarGridSpec(
            num_scalar_prefetch=3, grid=(D // tn, nb, F // tk),
            in_specs=[
                pl.BlockSpec((tm, tk), lambda n, i, k, be, bv, nr: (row(i, nr), k)),
                pl.BlockSpec((None, tk, tn), lambda n, i, k, be, bv, nr: (be[row(i, nr)], k, n)),
            ],
            out_specs=pl.BlockSpec((tm, tn), lambda n, i, k, be, bv, nr: (i, n))),
        out_shape=jax.ShapeDtypeStruct((P, D), F32),
        compiler_params=_cparams("arbitrary", "arbitrary", "arbitrary"),
        name="glu_down",
    )(block_e, block_valid, n_real, h, wd)


def _rot_half_cols(w):
    half = QK_ROPE_DIM // 2
    return jnp.concatenate([-w[..., half:], w[..., :half]], axis=-1)


def _rope_table(S):
    inv = 1.0 / (ROPE_THETA ** (jnp.arange(0, QK_ROPE_DIM, 2, dtype=F32) / QK_ROPE_DIM))
    ang = jnp.arange(S, dtype=F32)[:, None] * inv[None, :]
    c, s = jnp.cos(ang), jnp.sin(ang)
    return jnp.concatenate([c, c, s, s], axis=1)


def _tile_maps(seqs, tm):
    seq_of_tile, tile_pos = [], []
    for sid, (row0, length) in enumerate(seqs):
        assert row0 % tm == 0 and length % tm == 0
        for t in range(length // tm):
            seq_of_tile.append(sid)
            tile_pos.append(t)
    return jnp.asarray(np.array(seq_of_tile, np.int32)), jnp.asarray(np.array(tile_pos, np.int32))


def _moe_plan(logits, n_experts, rows):
    T = logits.shape[0]
    A = T * TOP_K
    top_vals, top_idx = lax.top_k(logits, TOP_K)
    probs = jax.nn.softmax(top_vals, axis=-1)
    flat_e = top_idx.reshape(A).astype(jnp.int32)
    flat_tok = jnp.repeat(jnp.arange(T, dtype=jnp.int32), TOP_K)
    order = jnp.argsort(flat_e)
    sorted_e = flat_e[order]
    counts = jnp.bincount(flat_e, length=n_experts).astype(jnp.int32)
    padded = (counts + rows - 1) // rows * rows
    padded_end = jnp.cumsum(padded)
    start = jnp.cumsum(counts) - counts
    start_pad = padded_end - padded
    dest = start_pad[sorted_e] + jnp.arange(A, dtype=jnp.int32) - start[sorted_e]
    n_blocks = (A + n_experts * (rows - 1) + rows - 1) // rows
    P = n_blocks * rows
    pos = dest[jnp.argsort(order)].reshape(T, TOP_K)
    block_start = jnp.arange(n_blocks, dtype=jnp.int32) * rows
    block_e = jnp.minimum(jnp.searchsorted(padded_end, block_start, side="right"), n_experts - 1).astype(jnp.int32)
    slot = jnp.arange(P, dtype=jnp.int32)
    slot_e = jnp.repeat(block_e, rows)
    k_in = slot - start_pad[slot_e]
    src = jnp.clip(start[slot_e] + k_in, 0, A - 1)
    slot_tok = jnp.where(k_in < counts[slot_e], flat_tok[order][src], 0)
    group_end = start_pad + counts
    block_valid = jnp.clip(group_end[block_e] - block_start, 0, rows)
    block_valid = jnp.where(block_start < padded_end[-1], block_valid, 0).astype(jnp.int32)
    n_real = (padded_end[-1] // rows).astype(jnp.int32)
    return slot_tok, pos, probs, block_e, block_valid, n_real


def kernel(x_prompt, x_sample, c_prompt, c_sample, w_ada, b_ada, norm_mix_pre, norm_mix_post, norm_ffn_pre, norm_ffn_post, w_in, q_norm, w_q_up, kv_norm, w_kv_up, conv_w, conv_b, w_rg_a, b_rg_a, w_rg_i, b_rg_i, rg_lambda, attn_out_norm, rnn_out_norm, w_out, w_ff_gate, w_ff_up, w_ff_down, w_router, w_exp_gate, w_exp_up, w_exp_down):
    B, S, D = x_prompt.shape
    DB, DS, _ = x_sample.shape
    depth = w_in.shape[0]
    T0, T1 = B * S, DB * DS
    T = T0 + T1
    q_rank = q_norm.shape[1]
    kv_rank = kv_norm.shape[1]
    H = w_q_up.shape[2] // (QK_NOPE_DIM + QK_ROPE_DIM)
    C = conv_w.shape[2]
    n_rnn_blocks = w_rg_a.shape[2]
    assert w_rg_a.shape[3] == LANES and C == n_rnn_blocks * LANES
    F_ff = w_ff_gate.shape[2]
    E = w_router.shape[2]
    scale = (QK_NOPE_DIM + QK_ROPE_DIM) ** -0.5

    seqs = [(b * S, S) for b in range(B)] + [(T0 + b * DS, DS) for b in range(DB)]
    s_min = min(S, DS)
    tm = min(1024, s_min)
    te = min(256, s_min)
    _, pos_tm = _tile_maps(seqs, tm)
    seq_te, _ = _tile_maps(seqs, te)
    cs_table = _rope_table(max(S, DS))

    xs = [x_prompt.reshape(T0, D), x_sample.reshape(T1, D)]
    n_seq = B + DB
    c_all = jnp.concatenate([c_prompt, c_sample, jnp.zeros((-n_seq % 16, D), F32)], axis=0)

    def vec(p):
        return p.reshape(1, -1)

    tn_ada = min(1024, 6 * D)
    mod = fk_matmul("ada", [c_all], [w_ada], w_lead_tiles=6 * D // tn_ada, prologue="silu",
                    bias=b_ada.reshape(1, depth * 6 * D), tm=c_all.shape[0], tn=tn_ada, out_dtype=F32)
    mod = mod.reshape(c_all.shape[0], depth * 6, D).transpose(1, 0, 2)

    tq = min(1024, s_min)
    rows = min(MOE_ROWS, s_min)
    tn_ff = min(512, F_ff)
    tk_ff = min(1024, F_ff)
    tn_dn = min(2048, D)

    hmix = prenorm(xs, vec(norm_mix_pre[0]), mod, seq_te, tm=te, sc=1, sh=0)

    for l in range(depth):
        m0 = 6 * l
        wl = w_in[l]
        n_lat = q_rank + kv_rank
        w_kr = wl[:, n_lat:n_lat + QK_ROPE_DIM]
        tn_qkv = next(t for t in (512, 256, 128) if n_lat % t == 0)
        assert q_rank % kv_rank == 0
        w_r = wl[:, n_lat + QK_ROPE_DIM:]
        qkv = fk_matmul("qkv_proj", [hmix], [wl[:, :n_lat]], tm=tm, tn=tn_qkv, out_dtype=F32)
        k_rope = fk_matmul("k_rope", [hmix], [jnp.concatenate([w_kr, _rot_half_cols(w_kr)], axis=1)],
                           tm=tm, tn=LANES, out_dtype=BF16, epilogue="rope", cs=cs_table, tile_pos=pos_tm)
        r = fk_matmul("r_proj", [hmix], [w_r], tm=tm, tn=min(512, 2 * C), out_dtype=F32)

        wq = w_q_up[l].reshape(q_rank, H, QK_NOPE_DIM + QK_ROPE_DIM)
        w_qn = wq[:, :, :QK_NOPE_DIM].reshape(q_rank, H * QK_NOPE_DIM)
        wq_r = wq[:, :, QK_NOPE_DIM:]
        w_qr = jnp.concatenate([wq_r, _rot_half_cols(wq_r)], axis=-1).reshape(q_rank, H * LANES)
        wkv = w_kv_up[l].reshape(kv_rank, H, QK_NOPE_DIM + V_HEAD_DIM)
        w_kv = jnp.concatenate([wkv[:, :, :QK_NOPE_DIM].reshape(kv_rank, H * QK_NOPE_DIM),
                                wkv[:, :, QK_NOPE_DIM:].reshape(kv_rank, H * V_HEAD_DIM)], axis=1)
        qg = vec(q_norm[l])
        q_cols = [(q_rank, 0)]
        kv_cols = [(kv_rank, q_rank // kv_rank)]
        tn_q = min(1024, w_qn.shape[1])
        q_up = fk_matmul("q_up", [qkv], [jnp.concatenate([w_qn, w_qr], axis=1)], x_cols=q_cols, gs=[qg],
                         prologue="rms", tm=tm, tn=tn_q, out_dtype=BF16, epilogue="rope",
                         rope_from=w_qn.shape[1] // tn_q, cs=cs_table, tile_pos=pos_tm)
        kv = fk_matmul("kv_up", [qkv], [w_kv], x_cols=kv_cols, gs=[vec(kv_norm[l])], prologue="rms", tm=tm,
                       tn=min(1024, w_kv.shape[1]), out_dtype=BF16)

        y_att = jnp.zeros((T, H * V_HEAD_DIM), BF16)
        y_att = attention(q_up, kv, k_rope, y_att, n_heads=H, n_rows=T, batch=B, seq=S, row0=0,
                          tq=min(tq, S), scale=scale, name="attn_prompt")
        y_att = attention(q_up, kv, k_rope, y_att, n_heads=H, n_rows=T, batch=DB, seq=DS, row0=T0,
                          tq=min(tq, DS), scale=scale, name="attn_sample")

        rp = jnp.concatenate([conv_w[l], conv_b[l][None], 0.5 * b_rg_a[l], 0.5 * b_rg_i[l], rg_lambda[l],
                              jnp.zeros((5, C), F32)], axis=0)
        w_cat = 0.5 * jnp.concatenate([w_rg_a[l, 0], w_rg_a[l, 1], w_rg_i[l, 0], w_rg_i[l, 1]], axis=-1)
        tc = min(256, C)
        y_rnn = jnp.zeros((T, C), BF16)
        y_rnn = rglru(r, rp, w_cat, y_rnn, n_rows=T, batch=B, seq=S, row0=0, tc=tc, name="rglru_prompt")
        y_rnn = rglru(r, rp, w_cat, y_rnn, n_rows=T, batch=DB, seq=DS, row0=T0, tc=tc, name="rglru_sample")

        assert w_out.shape[1] == 2 * y_att.shape[1] and y_rnn.shape[1] == y_att.shape[1]
        y_mix = fk_matmul("out_proj", [y_att, y_rnn], [w_out, w_out], w_lead=l, w_rows=(0, 1),
                          gs=[vec(attn_out_norm[l]), vec(rnn_out_norm[l])], prologue="rms",
                          tm=tm, tn=min(512, D), out_dtype=F32)

        j = l // 2
        moe = None
        if l % 2 == 0:
            x, hff = post_residual(xs, y_mix, vec(norm_mix_post[l]), mod, seq_te, tm=te, gate=m0 + 2,
                                   gpre=vec(norm_ffn_pre[l]), sc=m0 + 4, sh=m0 + 3)
            nb = T // rows
            be = jnp.full((nb,), j, jnp.int32)
            bv = jnp.full((nb,), rows, jnp.int32)
            nr = jnp.full((1,), nb, jnp.int32)
            hidden = glu_up(hff, w_ff_gate, w_ff_up, be, bv, nr, tm=rows, tn=tn_ff)
            f = glu_down(hidden, w_ff_down, be, bv, nr, tm=rows, tn=tn_dn, tk=tk_ff)
        else:
            w_r_pad = jnp.concatenate([w_router[j], jnp.zeros((D, LANES - E), F32)], axis=1)
            x, hff, logits = post_residual(xs, y_mix, vec(norm_mix_post[l]), mod, seq_te, tm=te, gate=m0 + 2,
                                           gpre=vec(norm_ffn_pre[l]), sc=m0 + 4, sh=m0 + 3, w_router=w_r_pad,
                                           pack=True)
            slot_tok, pos, probs, be, bv, nr = _moe_plan(logits[:, :E], E, rows)
            be = be + j * E
            nr = nr.reshape(1)
            x_sorted = gather_rows(hff, slot_tok, rows=rows)
            n_moe = w_exp_gate.shape[0]
            hidden = glu_up(x_sorted, w_exp_gate.reshape(n_moe * E, D, F_ff),
                            w_exp_up.reshape(n_moe * E, D, F_ff), be, bv, nr, tm=rows, tn=tn_ff, packed=True)
            ys = glu_down(hidden, w_exp_down.reshape(n_moe * E, F_ff, D), be, bv, nr,
                          tm=rows, tn=tn_dn, tk=tk_ff)
            f, moe = None, (ys, pos, probs)
        xs = [x]

        if l + 1 < depth:
            x, hmix = post_residual(xs, f, vec(norm_ffn_post[l]), mod, seq_te, tm=te, gate=m0 + 5, moe=moe,
                                    gpre=vec(norm_mix_pre[l + 1]), sc=m0 + 7, sh=m0 + 6)
            xs = [x]
        else:
            y_prompt, y_sample = post_residual(xs, f, vec(norm_ffn_post[l]), mod, seq_te, tm=te, gate=m0 + 5,
                                               moe=moe, out_rows=(T0, T1))

    return (y_prompt.reshape(B, S, D), y_sample.reshape(DB, DS, D))
```
